```python
import jax, jax.numpy as jnp
from jax import lax
import numpy as np

D_MODEL = 1024
BATCH = 4
SEQ = 4096
DEPTH = 4
DEC_BATCH = 128
DEC_SEQ = 4
PAST_LEN = 2048
PAGE_SIZE = 128

HEAD_DIM = 64
D_MIX = D_MODEL
MEM_WIDTH = D_MIX // 4
MEM_HEADS = MEM_WIDTH // HEAD_DIM
D_TOK = D_MIX - MEM_WIDTH
FOX_HEADS = D_TOK // HEAD_DIM
D_LRU = D_TOK
LRU_BLOCKS = FOX_HEADS
LRU_BLOCK_DIM = D_LRU // LRU_BLOCKS
CONV_WIDTH = 4
LRU_C = 8.0
D_FF = 4 * D_MODEL
N_MEM = 256
QBLOCK = 128
N_MIXERS = 2
N_FOX = (DEPTH + 1) // 2
N_LRU = DEPTH // 2
FOX_IN_COLS = 3 * D_TOK + FOX_HEADS + MEM_WIDTH
LRU_IN_COLS = 2 * D_LRU + MEM_WIDTH
FOX_FORGET_BIAS = 4.0
EPS = 1e-6

kernel_name = 'hybrid_fox_rglru_mem_decoder_step'


def rmsnorm(x, g):
    xf = x.astype(jnp.float32)
    y = xf * lax.rsqrt(jnp.mean(xf * xf, axis=-1, keepdims=True) + EPS) * g.astype(jnp.float32)
    return y.astype(x.dtype)


def fox_attend_block(q, k, v, c_q, c_k, mask):
    logits = jnp.einsum('bqhd,bkhd->bhqk', q, k).astype(jnp.float32) * (HEAD_DIM ** -0.5)
    logits = logits + (jnp.swapaxes(c_q, 1, 2)[:, :, :, None] - jnp.swapaxes(c_k, 1, 2)[:, :, None, :])
    logits = jnp.where(mask, logits, -jnp.inf)
    p = jax.nn.softmax(logits, axis=-1)
    return jnp.einsum('bhqk,bkhd->bqhd', p.astype(v.dtype), v)


def fox_prompt(q, k, v, logf):
    b, s, h, d = q.shape
    nb = s // QBLOCK
    c = jnp.cumsum(logf.astype(jnp.float32), axis=1)
    qb = jnp.moveaxis(q.reshape(b, nb, QBLOCK, h, d), 1, 0)
    cb = jnp.moveaxis(c.reshape(b, nb, QBLOCK, h), 1, 0)
    starts = jnp.arange(nb, dtype=jnp.int32) * QBLOCK
    kpos = jnp.arange(s, dtype=jnp.int32)

    def one_block(args):
        q_blk, c_blk, start = args
        qpos = start + jnp.arange(QBLOCK, dtype=jnp.int32)
        mask = kpos[None, :] <= qpos[:, None]
        return fox_attend_block(q_blk, k, v, c_blk, c, mask)

    out = lax.map(one_block, (qb, cb, starts))
    return jnp.moveaxis(out, 0, 1).reshape(b, s, h * d)


def fox_sample(q, k, v, logf, past_k, past_v, past_logf):
    b, t, h, d = q.shape
    p_len = past_k.shape[1]
    k_all = jnp.concatenate([past_k.astype(k.dtype), k], axis=1)
    v_all = jnp.concatenate([past_v.astype(v.dtype), v], axis=1)
    c_all = jnp.cumsum(jnp.concatenate([past_logf.astype(jnp.float32), logf.astype(jnp.float32)], axis=1), axis=1)
    kpos = jnp.arange(p_len + t, dtype=jnp.int32)
    qpos = p_len + jnp.arange(t, dtype=jnp.int32)
    mask = kpos[None, :] <= qpos[:, None]
    out = fox_attend_block(q, k_all, v_all, c_all[:, p_len:], c_all, mask)
    return out.reshape(b, t, h * d)


def fox_project(h, w_in, b_f):
    z = h @ w_in
    b, t, _ = z.shape
    q = z[..., :D_TOK].reshape(b, t, FOX_HEADS, HEAD_DIM)
    k = z[..., D_TOK:2 * D_TOK].reshape(b, t, FOX_HEADS, HEAD_DIM)
    v = z[..., 2 * D_TOK:3 * D_TOK].reshape(b, t, FOX_HEADS, HEAD_DIM)
    logf = jax.nn.log_sigmoid((z[..., 3 * D_TOK:3 * D_TOK + FOX_HEADS] + b_f).astype(jnp.float32))
    cq = z[..., 3 * D_TOK + FOX_HEADS:]
    return q, k, v, logf, cq


def causal_conv(xb, buf, w, bias):
    t = xb.shape[1]
    xp = jnp.concatenate([buf.astype(xb.dtype), xb], axis=1)
    y = bias + sum(xp[:, i:i + t] * w[i] for i in range(CONV_WIDTH))
    return y, xp[:, t:]


def rglru(xc, h0, w_r, b_r, w_i, b_i, lam):
    b, t, _ = xc.shape
    xh = xc.reshape(b, t, LRU_BLOCKS, LRU_BLOCK_DIM)
    r = jax.nn.sigmoid((jnp.einsum('bthi,hij->bthj', xh, w_r).reshape(b, t, D_LRU) + b_r).astype(jnp.float32))
    ig = jax.nn.sigmoid((jnp.einsum('bthi,hij->bthj', xh, w_i).reshape(b, t, D_LRU) + b_i).astype(jnp.float32))
    log_a = -LRU_C * r * jax.nn.softplus(-lam.astype(jnp.float32))
    a = jnp.exp(log_a)
    u = jnp.sqrt(-jnp.expm1(2.0 * log_a)) * ig * xc.astype(jnp.float32)

    def step(hc, au):
        a_t, u_t = au
        hc = a_t * hc + u_t
        return hc, hc

    h_last, hs = lax.scan(step, h0.astype(jnp.float32), (jnp.swapaxes(a, 0, 1), jnp.swapaxes(u, 0, 1)))
    return jnp.swapaxes(hs, 0, 1).astype(xc.dtype), h_last


def lru_mixer(h, w_in, conv_w, conv_b, w_r, b_r, w_i, b_i, lam, h0, buf):
    z = h @ w_in
    gate = z[..., :D_LRU]
    xb = z[..., D_LRU:2 * D_LRU]
    cq = z[..., 2 * D_LRU:]
    xc, new_buf = causal_conv(xb, buf, conv_w, conv_b)
    hs, h_last = rglru(xc, h0, w_r, b_r, w_i, b_i, lam)
    return jax.nn.gelu(gate) * hs, cq, h_last, new_buf


def mem_kv(mem, g, w_kv):
    kv = rmsnorm(mem, g) @ w_kv
    b, n, _ = kv.shape
    return (kv[..., :MEM_WIDTH].reshape(b, n, MEM_HEADS, HEAD_DIM),
            kv[..., MEM_WIDTH:].reshape(b, n, MEM_HEADS, HEAD_DIM))


def cross_attend(cq, mem_k, mem_v):
    b, t, _ = cq.shape
    q = cq.reshape(b, t, MEM_HEADS, HEAD_DIM)
    logits = jnp.einsum('bqhd,bkhd->bhqk', q, mem_k.astype(q.dtype)).astype(jnp.float32) * (HEAD_DIM ** -0.5)
    p = jax.nn.softmax(logits, axis=-1)
    return jnp.einsum('bhqk,bkhd->bqhd', p.astype(q.dtype), mem_v.astype(q.dtype)).reshape(b, t, MEM_WIDTH)


def finish_layer(x, y_tok, cq, mem_k, mem_v, w_out, g_mlp, w_up, w_down):
    y = jnp.concatenate([y_tok, cross_attend(cq, mem_k, mem_v)], axis=-1)
    x = x + y @ w_out
    hm = rmsnorm(x, g_mlp)
    return x + jnp.square(jax.nn.relu(hm @ w_up)) @ w_down


def setup_inputs(seed: int = 0) -> dict:
    key = jax.random.key(seed)
    ks = jax.random.split(key, 32)
    f32 = jnp.float32
    n_pages = PAST_LEN // PAGE_SIZE
    n_used = DEC_BATCH * n_pages
    n_pool = n_used + max(1, n_used // 4)

    def nrm(k, shape, scale=1.0):
        return scale * jax.random.normal(k, shape, f32)

    def gain(k, shape):
        return 1.0 + nrm(k, shape, 0.05)

    a0 = jax.random.uniform(ks[24], (N_LRU, D_LRU), f32, 0.9, 0.999)
    s0 = a0 ** (1.0 / LRU_C)
    return {
        'x_prompt': nrm(ks[0], (BATCH, SEQ, D_MODEL)),
        'x_sample': nrm(ks[1], (DEC_BATCH, DEC_SEQ, D_MODEL)),
        'mem_prompt': nrm(ks[2], (BATCH, N_MEM, D_MODEL)),
        'cache_fox_k': nrm(ks[3], (N_FOX, n_pool, PAGE_SIZE, FOX_HEADS, HEAD_DIM)),
        'cache_fox_v': nrm(ks[4], (N_FOX, n_pool, PAGE_SIZE, FOX_HEADS, HEAD_DIM)),
        'cache_fox_logf': jax.nn.log_sigmoid(FOX_FORGET_BIAS + nrm(ks[5], (N_FOX, n_pool, PAGE_SIZE, FOX_HEADS))),
        'state_lru_h': nrm(ks[6], (N_LRU, DEC_BATCH, D_LRU), 0.5),
        'state_lru_conv': nrm(ks[7], (N_LRU, DEC_BATCH, CONV_WIDTH - 1, D_LRU)),
        'cache_mem_k': nrm(ks[8], (DEPTH, DEC_BATCH, N_MEM, MEM_HEADS, HEAD_DIM)),
        'cache_mem_v': nrm(ks[9], (DEPTH, DEC_BATCH, N_MEM, MEM_HEADS, HEAD_DIM)),
        'page_table': jax.random.permutation(ks[10], n_pool)[:n_used].reshape(DEC_BATCH, n_pages).astype(jnp.int32),
        'norm_mix': gain(ks[11], (DEPTH, D_MODEL)),
        'norm_mem': gain(ks[12], (DEPTH, D_MODEL)),
        'norm_mlp': gain(ks[13], (DEPTH, D_MODEL)),
        'norm_final': gain(ks[14], (D_MODEL,)),
        'w_in_fox': nrm(ks[15], (N_FOX, D_MODEL, FOX_IN_COLS), D_MODEL ** -0.5),
        'b_forget': FOX_FORGET_BIAS + nrm(ks[16], (N_FOX, FOX_HEADS), 0.5),
        'w_in_lru': nrm(ks[17], (N_LRU, D_MODEL, LRU_IN_COLS), D_MODEL ** -0.5),
        'conv_w': nrm(ks[18], (N_LRU, CONV_WIDTH, D_LRU), CONV_WIDTH ** -0.5),
        'conv_b': nrm(ks[19], (N_LRU, D_LRU), 0.02),
        'w_rgate': nrm(ks[20], (N_LRU, LRU_BLOCKS, LRU_BLOCK_DIM, LRU_BLOCK_DIM), LRU_BLOCK_DIM ** -0.5),
        'b_rgate': nrm(ks[21], (N_LRU, D_LRU), 0.02),
        'w_igate': nrm(ks[22], (N_LRU, LRU_BLOCKS, LRU_BLOCK_DIM, LRU_BLOCK_DIM), LRU_BLOCK_DIM ** -0.5),
        'b_igate': nrm(ks[23], (N_LRU, D_LRU), 0.02),
        'lru_lambda': jnp.log(s0) - jnp.log1p(-s0),
        'w_mem_kv': nrm(ks[25], (DEPTH, D_MODEL, 2 * MEM_WIDTH), D_MODEL ** -0.5),
        'w_out': nrm(ks[26], (DEPTH, D_MIX, D_MODEL), D_MIX ** -0.5),
        'w_up': nrm(ks[27], (DEPTH, D_MODEL, D_FF), D_MODEL ** -0.5),
        'w_down': nrm(ks[28], (DEPTH, D_FF, D_MODEL), 0.5 * D_FF ** -0.5),
    }


def reference(x_prompt, x_sample, mem_prompt, cache_fox_k, cache_fox_v, cache_fox_logf,
              state_lru_h, state_lru_conv, cache_mem_k, cache_mem_v, page_table,
              norm_mix, norm_mem, norm_mlp, norm_final, w_in_fox, b_forget, w_in_lru,
              conv_w, conv_b, w_rgate, b_rgate, w_igate, b_igate, lru_lambda,
              w_mem_kv, w_out, w_up, w_down):
    db = page_table.shape[0]
    bp = x_prompt.shape[0]
    xp, xs = x_prompt, x_sample
    pk, pv, pf, sk, sv, sf = [], [], [], [], [], []
    ph, pc, sh, sc = [], [], [], []
    mk_out, mv_out = [], []
    for l in range(DEPTH):
        j = l // N_MIXERS
        mk_p, mv_p = mem_kv(mem_prompt, norm_mem[l], w_mem_kv[l])
        mk_out.append(mk_p)
        mv_out.append(mv_p)
        hp = rmsnorm(xp, norm_mix[l])
        hs = rmsnorm(xs, norm_mix[l])
        if l % N_MIXERS == 0:
            qp, kp, vp, fp, cqp = fox_project(hp, w_in_fox[j], b_forget[j])
            yp = fox_prompt(qp, kp, vp, fp)
            qs, ks_, vs, fs, cqs = fox_project(hs, w_in_fox[j], b_forget[j])
            past_k = cache_fox_k[j][page_table].reshape(db, -1, FOX_HEADS, HEAD_DIM)
            past_v = cache_fox_v[j][page_table].reshape(db, -1, FOX_HEADS, HEAD_DIM)
            past_f = cache_fox_logf[j][page_table].reshape(db, -1, FOX_HEADS)
            ys = fox_sample(qs, ks_, vs, fs, past_k, past_v, past_f)
            pk.append(kp); pv.append(vp); pf.append(fp)
            sk.append(ks_); sv.append(vs); sf.append(fs)
        else:
            h0_p = jnp.zeros((bp, D_LRU), jnp.float32)
            buf_p = jnp.zeros((bp, CONV_WIDTH - 1, D_LRU), xp.dtype)
            yp, cqp, hl_p, nb_p = lru_mixer(hp, w_in_lru[j], conv_w[j], conv_b[j], w_rgate[j], b_rgate[j],
                                            w_igate[j], b_igate[j], lru_lambda[j], h0_p, buf_p)
            ys, cqs, hl_s, nb_s = lru_mixer(hs, w_in_lru[j], conv_w[j], conv_b[j], w_rgate[j], b_rgate[j],
                                            w_igate[j], b_igate[j], lru_lambda[j], state_lru_h[j], state_lru_conv[j])
            ph.append(hl_p); pc.append(nb_p); sh.append(hl_s); sc.append(nb_s)
        xp = finish_layer(xp, yp, cqp, mk_p, mv_p, w_out[l], norm_mlp[l], w_up[l], w_down[l])
        xs = finish_layer(xs, ys, cqs, cache_mem_k[l], cache_mem_v[l], w_out[l], norm_mlp[l], w_up[l], w_down[l])
    y_prompt = rmsnorm(xp, norm_final)
    y_sample = rmsnorm(xs, norm_final)
    return (y_prompt, y_sample,
            jnp.stack(pk), jnp.stack(pv), jnp.stack(pf),
            jnp.stack(sk), jnp.stack(sv), jnp.stack(sf),
            jnp.stack(ph), jnp.stack(pc), jnp.stack(sh), jnp.stack(sc),
            jnp.stack(mk_out), jnp.stack(mv_out))
```

```python
import functools

import jax
import jax.numpy as jnp
from jax import lax
from jax.experimental import pallas as pl
from jax.experimental.pallas import tpu as pltpu

F32 = jnp.float32
BF16 = jnp.bfloat16

HEAD_DIM = 64
D_TOK = 768
MEM_WIDTH = 256
FOX_HEADS = D_TOK // HEAD_DIM
MEM_HEADS = MEM_WIDTH // HEAD_DIM
HEADS_PAD = 16
CONV_WIDTH = 4
LRU_C = 8.0
GATE_BLOCK = 256
EPS = 1e-6
SCALE = HEAD_DIM ** -0.5
NEG = -1e30
LANES = 128
SUBLANES = 8
ROW_TILE = 512
FF_CHUNK = 1024
VMEM_LIMIT = 56 * 1024 * 1024


def _dot(a, b):
    return jnp.dot(a, b, preferred_element_type=F32)


def _dot_nt(a, b):
    return lax.dot_general(a, b, (((1,), (1,)), ((), ())), preferred_element_type=F32)


def _rmsnorm(x, g):
    return x * lax.rsqrt(jnp.mean(x * x, axis=-1, keepdims=True) + EPS) * g


def _softplus(x):
    return jnp.maximum(x, 0.0) + jnp.log1p(jnp.exp(-jnp.abs(x)))


def _log_sigmoid(x):
    return -_softplus(-x)


def _gelu_tanh(x):
    return x * (0.5 * (1.0 + jnp.tanh(0.7978845608028654 * (x + 0.044715 * (x * x * x)))))


def _tri_ones():
    r = lax.broadcasted_iota(jnp.int32, (LANES, 2 * LANES), 0)
    c = lax.broadcasted_iota(jnp.int32, (LANES, 2 * LANES), 1)
    return jnp.where((c >= LANES) | (r <= c), 1.0, 0.0).astype(BF16)


def _dot_f32_lhs(x, m):
    hi = x.astype(BF16)
    r1 = x - hi.astype(F32)
    mid = r1.astype(BF16)
    lo = (r1 - mid.astype(F32)).astype(BF16)
    return _dot(hi, m) + _dot(mid, m) + _dot(lo, m)


def _const_spec(shape):
    nd = len(shape)
    return pl.BlockSpec(shape, lambda *_: (0,) * nd, pipeline_mode=pl.Buffered(1))


def _params(n_axes):
    return pltpu.CompilerParams(dimension_semantics=("arbitrary",) * n_axes,
                                vmem_limit_bytes=VMEM_LIMIT)


def _memkv_kernel(m_ref, g_ref, w_ref, k_ref, v_ref):
    hm = _rmsnorm(m_ref[0], g_ref[0]).astype(BF16)
    kvt = _dot_nt(w_ref[0], hm)
    k_ref[0, 0] = kvt[:MEM_WIDTH]
    v_ref[0, 0] = kvt[MEM_WIDTH:]


def _mem_kv(mem, norm_mem, w_kv_t):
    depth = norm_mem.shape[0]
    b, n_mem, d = mem.shape
    out = jax.ShapeDtypeStruct((depth, b, MEM_WIDTH, n_mem), F32)
    return pl.pallas_call(
        _memkv_kernel,
        grid=(depth, b),
        in_specs=[pl.BlockSpec((1, n_mem, d), lambda l, i: (i, 0, 0)),
                  pl.BlockSpec((1, 1, d), lambda l, i: (l, 0, 0)),
                  pl.BlockSpec((1, 2 * MEM_WIDTH, d), lambda l, i: (l, 0, 0))],
        out_specs=[pl.BlockSpec((1, 1, MEM_WIDTH, n_mem), lambda l, i: (l, i, 0, 0))] * 2,
        out_shape=[out, out],
        compiler_params=_params(2),
        name="mem_kv",
    )(mem, norm_mem.reshape(depth, 1, d), w_kv_t)


def _fox_in_prompt_kernel(x_ref, g_ref, wq_ref, wkvt_ref, wft_ref, bf_ref, wcq_ref,
                          q_ref, kt_ref, vt_ref, lft_ref, ct_ref, cq_ref, carry_ref, *, tm):
    @pl.when(pl.program_id(1) == 0)
    def _():
        carry_ref[...] = jnp.zeros_like(carry_ref)

    h = _rmsnorm(x_ref[0], g_ref[...]).astype(BF16)
    q_ref[0] = _dot(h, wq_ref[...]).astype(BF16)
    kvt = _dot_nt(wkvt_ref[...], h)
    kt_ref[0] = kvt[:D_TOK]
    vt_ref[0] = kvt[D_TOK:]
    lf = _log_sigmoid(_dot_nt(wft_ref[...], h) + bf_ref[...])
    lft_ref[0] = lf
    cq_ref[0] = _dot(h, wcq_ref[...]).astype(BF16)
    uo = _tri_ones()
    carry = carry_ref[...]
    pieces = []
    for blk in range(tm // LANES):
        r = _dot_f32_lhs(lf[:, blk * LANES:(blk + 1) * LANES], uo)
        pieces.append(carry + r[:, :LANES])
        carry = carry + r[:, LANES:]
    ct_ref[0] = jnp.concatenate(pieces, axis=1)
    carry_ref[...] = carry


def _fox_in_prompt(x, g, wq, wkvt, wft, bf, wcq, tm):
    b, s, d = x.shape
    tok = lambda w: pl.BlockSpec((1, tm, w), lambda i, j: (i, j, 0))
    tr = lambda r: pl.BlockSpec((1, r, tm), lambda i, j: (i, 0, j))
    return pl.pallas_call(
        functools.partial(_fox_in_prompt_kernel, tm=tm),
        grid=(b, s // tm),
        in_specs=[tok(d), _const_spec((1, d)), _const_spec(wq.shape), _const_spec(wkvt.shape),
                  _const_spec(wft.shape), _const_spec(bf.shape), _const_spec(wcq.shape)],
        out_specs=[tok(D_TOK), tr(D_TOK), tr(D_TOK), tr(HEADS_PAD), tr(HEADS_PAD), tok(MEM_WIDTH)],
        out_shape=[jax.ShapeDtypeStruct((b, s, D_TOK), BF16),
                   jax.ShapeDtypeStruct((b, D_TOK, s), F32),
                   jax.ShapeDtypeStruct((b, D_TOK, s), F32),
                   jax.ShapeDtypeStruct((b, HEADS_PAD, s), F32),
                   jax.ShapeDtypeStruct((b, HEADS_PAD, s), F32),
                   jax.ShapeDtypeStruct((b, s, MEM_WIDTH), BF16)],
        scratch_shapes=[pltpu.VMEM((HEADS_PAD, LANES), F32)],
        compiler_params=_params(2),
        name="fox_in_prompt",
    )(x, g, wq, wkvt, wft, bf, wcq)


def _fox_in_sample_kernel(x_ref, g_ref, wq_ref, wkv_ref, wkvt_ref, wft_ref, bf_ref, wcq_ref,
                          q_ref, kn_ref, vn_ref, kt_ref, vt_ref, lft_ref, cq_ref, *, t_new, db):
    h = _rmsnorm(x_ref[...], g_ref[...]).astype(BF16)
    q_ref[...] = _dot(h, wq_ref[...])
    kv = _dot(h, wkv_ref[...])
    kn_ref[...] = kv[:, :D_TOK]
    vn_ref[...] = kv[:, D_TOK:]
    kvt = _dot_nt(wkvt_ref[...], h)
    for t in range(t_new):
        kt_ref[t] = kvt[:D_TOK, t * db:(t + 1) * db]
        vt_ref[t] = kvt[D_TOK:, t * db:(t + 1) * db]
    lft_ref[...] = _log_sigmoid(_dot_nt(wft_ref[...], h) + bf_ref[...])
    cq_ref[...] = _dot(h, wcq_ref[...])


def _fox_in_sample(x, g, wq, wkv, wkvt, wft, bf, wcq, t_new, db):
    n, d = x.shape
    ins = (x, g, wq, wkv, wkvt, wft, bf, wcq)
    outs = [jax.ShapeDtypeStruct((n, D_TOK), F32),
            jax.ShapeDtypeStruct((n, D_TOK), F32),
            jax.ShapeDtypeStruct((n, D_TOK), F32),
            jax.ShapeDtypeStruct((t_new, D_TOK, db), F32),
            jax.ShapeDtypeStruct((t_new, D_TOK, db), F32),
            jax.ShapeDtypeStruct((HEADS_PAD, n), F32),
            jax.ShapeDtypeStruct((n, MEM_WIDTH), F32)]
    full = lambda a: pl.BlockSpec(a.shape, lambda i: (0,) * len(a.shape))
    return pl.pallas_call(
        functools.partial(_fox_in_sample_kernel, t_new=t_new, db=db),
        grid=(1,),
        in_specs=[full(a) for a in ins],
        out_specs=[full(o) for o in outs],
        out_shape=outs,
        compiler_params=_params(1),
        name="fox_in_sample",
    )(*ins)


def _fox_flash_kernel(q_ref, kt_ref, vt_ref, c_ref, o_ref, kb_ref, vb_ref, *, tq):
    hp = pl.program_id(1)
    qi = pl.program_id(2)

    @pl.when(qi == 0)
    def _():
        kb_ref[...] = kt_ref[0].astype(BF16)
        vb_ref[...] = vt_ref[0].astype(BF16)

    q2 = q_ref[0]
    lane = lax.broadcasted_iota(jnp.int32, (tq, 2 * HEAD_DIM), 1)
    row = lax.broadcasted_iota(jnp.int32, (tq, tq), 0)
    col = lax.broadcasted_iota(jnp.int32, (tq, tq), 1)
    outs = []
    for hh in range(2):
        in_head = (lane >= HEAD_DIM) if hh else (lane < HEAD_DIM)
        qh = jnp.where(in_head, q2, jnp.zeros_like(q2))
        crow = 2 * hp + hh

        def step(j, carry, masked, qh=qh, crow=crow):
            m, l, acc = carry
            off = pl.multiple_of(j * tq, tq)
            s = _dot(qh, kb_ref[:, pl.ds(off, tq)]) * SCALE - c_ref[0, pl.ds(crow, 1), pl.ds(off, tq)]
            if masked:
                s = jnp.where(col <= row, s, NEG)
            m_new = jnp.maximum(m, jnp.max(s, axis=-1, keepdims=True))
            alpha = jnp.exp(m - m_new)
            p = jnp.exp(s - m_new)
            l = alpha * l + jnp.sum(p, axis=-1, keepdims=True)
            acc = alpha * acc + _dot_nt(p.astype(BF16), vb_ref[:, pl.ds(off, tq)])
            return m_new, l, acc

        init = (jnp.full((tq, 1), NEG, F32), jnp.zeros((tq, 1), F32),
                jnp.zeros((tq, 2 * HEAD_DIM), F32))
        carry = lax.fori_loop(0, qi, functools.partial(step, masked=False), init)
        _, l, acc = step(qi, carry, True)
        outs.append(acc * (1.0 / l))
    o_ref[0] = jnp.where(lane < HEAD_DIM, outs[0], outs[1]).astype(BF16)


def _fox_flash(q, kt, vt, ct, tq):
    b, s, _ = q.shape
    hw = 2 * HEAD_DIM
    return pl.pallas_call(
        functools.partial(_fox_flash_kernel, tq=tq),
        grid=(b, D_TOK // hw, s // tq),
        in_specs=[pl.BlockSpec((1, tq, hw), lambda i, h, j: (i, j, h)),
                  pl.BlockSpec((1, hw, s), lambda i, h, j: (i, h, 0)),
                  pl.BlockSpec((1, hw, s), lambda i, h, j: (i, h, 0)),
                  pl.BlockSpec((1, HEADS_PAD, s), lambda i, h, j: (i, 0, 0))],
        out_specs=pl.BlockSpec((1, tq, hw), lambda i, h, j: (i, j, h)),
        out_shape=jax.ShapeDtypeStruct((b, s, D_TOK), BF16),
        scratch_shapes=[pltpu.VMEM((hw, s), BF16), pltpu.VMEM((hw, s), BF16)],
        compiler_params=_params(3),
        name="fox_flash",
    )(q, kt, vt, ct)


def _fox_decode_kernel(pt_ref, q_ref, kn_ref, vn_ref, lfn_ref, *refs, n_pages, t_new):
    del pt_ref
    kp = refs[:n_pages]
    vp = refs[n_pages:2 * n_pages]
    lp = refs[2 * n_pages:3 * n_pages]
    o_ref = refs[3 * n_pages]
    rows = t_new * HEADS_PAD

    q8 = q_ref[0]
    hrow = lax.broadcasted_iota(jnp.int32, (HEADS_PAD, D_TOK), 0)
    hcol = lax.broadcasted_iota(jnp.int32, (HEADS_PAD, D_TOK), 1) // HEAD_DIM
    head_mask = jnp.where(hrow == hcol, 1.0, 0.0)
    qs = jnp.concatenate([jnp.broadcast_to(q8[t:t + 1, :], (HEADS_PAD, D_TOK)) * head_mask
                          for t in range(t_new)], axis=0)
    qsb = qs.astype(BF16)

    x = jnp.concatenate([lp[p][0, 0] for p in range(n_pages)], axis=0)
    r = _dot_f32_lhs(x, _tri_ones())
    offs = jnp.zeros((HEADS_PAD, LANES), F32)
    s_pages = []
    for p in range(n_pages):
        c_page = r[p * HEADS_PAD:(p + 1) * HEADS_PAD, :LANES] + offs
        offs = offs + r[p * HEADS_PAD:(p + 1) * HEADS_PAD, LANES:]
        kpg = kp[p][0, 0].reshape(D_TOK, LANES).astype(BF16)
        s = _dot(qsb, kpg) * SCALE
        s_pages.append(s - jnp.concatenate([c_page] * t_new, axis=0))

    kn = kn_ref[0]
    vn = vn_ref[0]
    lfn = lfn_ref[0]
    trow = lax.broadcasted_iota(jnp.int32, (rows, 1), 0) // HEADS_PAD
    c_new = offs[:, :1]
    s_new = []
    for t in range(t_new):
        c_new = c_new + lfn[:, t:t + 1]
        st = jnp.sum(qs * kn[t:t + 1, :], axis=-1, keepdims=True) * SCALE
        st = st - jnp.concatenate([c_new] * t_new, axis=0)
        s_new.append(jnp.where(trow >= t, st, NEG))

    mx = s_pages[0]
    for p in range(1, n_pages):
        mx = jnp.maximum(mx, s_pages[p])
    m = jnp.max(mx, axis=-1, keepdims=True)
    for t in range(t_new):
        m = jnp.maximum(m, s_new[t])

    lsum = jnp.zeros((rows, LANES), F32)
    o = jnp.zeros((rows, D_TOK), F32)
    for p in range(n_pages):
        pp = jnp.exp(s_pages[p] - m)
        lsum = lsum + pp
        vpg = vp[p][0, 0].reshape(D_TOK, LANES).astype(BF16)
        o = o + _dot_nt(pp.astype(BF16), vpg)
    l = jnp.sum(lsum, axis=-1, keepdims=True)
    for t in range(t_new):
        pn = jnp.exp(s_new[t] - m)
        l = l + pn
        o = o + pn.astype(BF16).astype(F32) * vn[t:t + 1, :]
    o = o * (1.0 / l)
    outs = [jnp.sum(o[t * HEADS_PAD:(t + 1) * HEADS_PAD] * head_mask, axis=0, keepdims=True)
            for t in range(t_new)]
    outs.append(jnp.zeros((SUBLANES - t_new, D_TOK), F32))
    o_ref[0] = jnp.concatenate(outs, axis=0)


def _fox_decode(page_table_flat, q8, kn8, vn8, lfn, ck, cv, clf, layer, n_pages, t_new):
    db = q8.shape[0]
    page = ck.shape[-1]
    tok = pl.BlockSpec((1, SUBLANES, D_TOK), lambda i, pt: (i, 0, 0))
    kv_spec = lambda p: pl.BlockSpec((1, 1, FOX_HEADS, HEAD_DIM, page),
                                     lambda i, pt: (layer, pt[i * n_pages + p], 0, 0, 0))
    lf_spec = lambda p: pl.BlockSpec((1, 1, HEADS_PAD, page),
                                     lambda i, pt: (layer, pt[i * n_pages + p], 0, 0))
    grid_spec = pltpu.PrefetchScalarGridSpec(
        num_scalar_prefetch=1,
        grid=(db,),
        in_specs=[tok, tok, tok, pl.BlockSpec((1, HEADS_PAD, SUBLANES), lambda i, pt: (i, 0, 0))]
        + [kv_spec(p) for p in range(n_pages)] + [kv_spec(p) for p in range(n_pages)]
        + [lf_spec(p) for p in range(n_pages)],
        out_specs=tok,
    )
    return pl.pallas_call(
        functools.partial(_fox_decode_kernel, n_pages=n_pages, t_new=t_new),
        grid_spec=grid_spec,
        out_shape=jax.ShapeDtypeStruct((db, SUBLANES, D_TOK), F32),
        compiler_params=_params(1),
        name="fox_decode",
    )(page_table_flat, q8, kn8, vn8, lfn, *([ck] * n_pages), *([cv] * n_pages), *([clf] * n_pages))


def _lru_gates(xc, gate_refs, lam_ref):
    wr_ref, br_ref, wi_ref, bi_ref = gate_refs
    xcb = xc.astype(BF16)
    nblk = D_TOK // GATE_BLOCK
    sl = lambda i: xcb[:, i * GATE_BLOCK:(i + 1) * GATE_BLOCK]
    r_pre = jnp.concatenate([_dot(sl(i), wr_ref[i]) for i in range(nblk)], axis=1) + br_ref[...]
    i_pre = jnp.concatenate([_dot(sl(i), wi_ref[i]) for i in range(nblk)], axis=1) + bi_ref[...]
    r = jax.nn.sigmoid(r_pre)
    ig = jax.nn.sigmoid(i_pre)
    log_a = -LRU_C * r * _softplus(-lam_ref[...])
    a = jnp.exp(log_a)
    u = jnp.sqrt(-jnp.tanh(log_a) * (a * a + 1.0)) * ig * xc
    return a, u


def _lru_prompt_kernel(x_ref, g_ref, w_ref, cw_ref, cb_ref, wr_ref, br_ref, wi_ref, bi_ref, lam_ref,
                       y_ref, cq_ref, hl_ref, cs_ref, xs_ref, hc_ref, *, tm):
    @pl.when(pl.program_id(1) == 0)
    def _():
        xs_ref[0:SUBLANES, :] = jnp.zeros((SUBLANES, D_TOK), F32)
        hc_ref[...] = jnp.zeros_like(hc_ref)

    h = _rmsnorm(x_ref[0], g_ref[...]).astype(BF16)
    z = _dot(h, w_ref[...])
    gate = z[:, :D_TOK]
    xb = z[:, D_TOK:2 * D_TOK]
    cq_ref[0] = z[:, 2 * D_TOK:].astype(BF16)
    xs_ref[SUBLANES:SUBLANES + tm, :] = xb
    xc = cb_ref[...] + xb * cw_ref[CONV_WIDTH - 1:CONV_WIDTH, :]
    for i in range(CONV_WIDTH - 1):
        start = SUBLANES - (CONV_WIDTH - 1) + i
        xc = xc + xs_ref[start:start + tm, :] * cw_ref[i:i + 1, :]
    cs_ref[0] = xs_ref[SUBLANES + tm - (CONV_WIDTH - 1):SUBLANES + tm, :]
    xs_ref[0:SUBLANES, :] = xs_ref[tm:tm + SUBLANES, :]

    a, u = _lru_gates(xc, (wr_ref, br_ref, wi_ref, bi_ref), lam_ref)
    rows = lax.broadcasted_iota(jnp.int32, (tm, D_TOK), 0)
    sh = 1
    while sh < tm:
        keep = rows >= sh
        a_sh = jnp.where(keep, pltpu.roll(a, sh, axis=0), 1.0)
        u_sh = jnp.where(keep, pltpu.roll(u, sh, axis=0), 0.0)
        u = u + a * u_sh
        a = a * a_sh
        sh *= 2
    hs = u + a * hc_ref[...]
    hc_ref[...] = hs[tm - 1:tm, :]
    hl_ref[0] = hs[tm - 1:tm, :]
    y_ref[0] = (_gelu_tanh(gate) * hs).astype(BF16)


def _lru_prompt(x, g, w, cw, cb, wr, br, wi, bi, lam, tm):
    b, s, d = x.shape
    tok = lambda wd: pl.BlockSpec((1, tm, wd), lambda i, j: (i, j, 0))
    consts = (g, w, cw, cb, wr, br, wi, bi, lam)
    return pl.pallas_call(
        functools.partial(_lru_prompt_kernel, tm=tm),
        grid=(b, s // tm),
        in_specs=[tok(d)] + [_const_spec(c.shape) for c in consts],
        out_specs=[tok(D_TOK), tok(MEM_WIDTH),
                   pl.BlockSpec((1, 1, D_TOK), lambda i, j: (i, 0, 0)),
                   pl.BlockSpec((1, CONV_WIDTH - 1, D_TOK), lambda i, j: (i, 0, 0))],
        out_shape=[jax.ShapeDtypeStruct((b, s, D_TOK), BF16),
                   jax.ShapeDtypeStruct((b, s, MEM_WIDTH), BF16),
                   jax.ShapeDtypeStruct((b, 1, D_TOK), F32),
                   jax.ShapeDtypeStruct((b, CONV_WIDTH - 1, D_TOK), F32)],
        scratch_shapes=[pltpu.VMEM((tm + SUBLANES, D_TOK), F32), pltpu.VMEM((1, D_TOK), F32)],
        compiler_params=_params(2),
        name="lru_prompt",
    )(x, *consts)


def _lru_sample_kernel(x_ref, g_ref, w_ref, cw_ref, cb_ref, wr_ref, br_ref, wi_ref, bi_ref, lam_ref,
                       h0_ref, buf_ref, y_ref, cq_ref, hl_ref, nb_ref, *, t_new, db):
    h = _rmsnorm(x_ref[...], g_ref[...]).astype(BF16)
    z = _dot(h, w_ref[...])
    gate = z[:, :D_TOK]
    xb = z[:, D_TOK:2 * D_TOK]
    cq_ref[...] = z[:, 2 * D_TOK:]
    xp = [buf_ref[i] for i in range(CONV_WIDTH - 1)] + [xb[t * db:(t + 1) * db] for t in range(t_new)]
    xc_t = []
    for t in range(t_new):
        acc = cb_ref[...] + xp[t] * cw_ref[0:1, :]
        for i in range(1, CONV_WIDTH):
            acc = acc + xp[t + i] * cw_ref[i:i + 1, :]
        xc_t.append(acc)
    for i in range(CONV_WIDTH - 1):
        nb_ref[i] = xp[t_new + i]
    xc = jnp.concatenate(xc_t, axis=0)
    a, u = _lru_gates(xc, (wr_ref, br_ref, wi_ref, bi_ref), lam_ref)
    hc = h0_ref[...]
    hs = []
    for t in range(t_new):
        hc = a[t * db:(t + 1) * db] * hc + u[t * db:(t + 1) * db]
        hs.append(hc)
    hl_ref[...] = hc
    y_ref[...] = _gelu_tanh(gate) * jnp.concatenate(hs, axis=0)


def _lru_sample(x, g, w, cw, cb, wr, br, wi, bi, lam, h0, buf, t_new, db):
    n, d = x.shape
    ins = (x, g, w, cw, cb, wr, br, wi, bi, lam, h0, buf)
    outs = [jax.ShapeDtypeStruct((n, D_TOK), F32),
            jax.ShapeDtypeStruct((n, MEM_WIDTH), F32),
            jax.ShapeDtypeStruct((db, D_TOK), F32),
            jax.ShapeDtypeStruct((CONV_WIDTH - 1, db, D_TOK), F32)]
    full = lambda a: pl.BlockSpec(a.shape, lambda i: (0,) * len(a.shape))
    return pl.pallas_call(
        functools.partial(_lru_sample_kernel, t_new=t_new, db=db),
        grid=(1,),
        in_specs=[full(a) for a in ins],
        out_specs=[full(o) for o in outs],
        out_shape=outs,
        compiler_params=_params(1),
        name="lru_sample",
    )(*ins)


def _cross_sample_kernel(q_ref, mk_ref, mv_ref, o_ref, *, t_new, group):
    hpad = SUBLANES
    hrow = lax.broadcasted_iota(jnp.int32, (hpad, MEM_WIDTH), 0)
    hcol = lax.broadcasted_iota(jnp.int32, (hpad, MEM_WIDTH), 1) // HEAD_DIM
    head_mask = jnp.where(hrow == hcol, 1.0, 0.0)
    for i in range(group):
        q8 = q_ref[i]
        qs = jnp.concatenate([jnp.broadcast_to(q8[t:t + 1, :], (hpad, MEM_WIDTH)) * head_mask
                              for t in range(t_new)], axis=0)
        s = _dot(qs.astype(BF16), mk_ref[0, i].astype(BF16)) * SCALE
        e = jnp.exp(s - jnp.max(s, axis=-1, keepdims=True))
        p = e * (1.0 / jnp.sum(e, axis=-1, keepdims=True))
        o = _dot_nt(p.astype(BF16), mv_ref[0, i].astype(BF16))
        outs = [jnp.sum(o[t * hpad:(t + 1) * hpad] * head_mask, axis=0, keepdims=True)
                for t in range(t_new)]
        outs.append(jnp.zeros((SUBLANES - t_new, MEM_WIDTH), F32))
        o_ref[i] = jnp.concatenate(outs, axis=0)


def _cross_sample(q8, cmk, cmv, layer, t_new):
    db = q8.shape[0]
    n_mem = cmk.shape[-1]
    group = SUBLANES
    tok = pl.BlockSpec((group, SUBLANES, MEM_WIDTH), lambda i: (i, 0, 0))
    kv = pl.BlockSpec((1, group, MEM_WIDTH, n_mem), lambda i: (layer, i, 0, 0))
    return pl.pallas_call(
        functools.partial(_cross_sample_kernel, t_new=t_new, group=group),
        grid=(db // group,),
        in_specs=[tok, kv, kv],
        out_specs=tok,
        out_shape=jax.ShapeDtypeStruct((db, SUBLANES, MEM_WIDTH), F32),
        compiler_params=_params(1),
        name="cross_sample",
    )(q8, cmk, cmv)


def _finish_kernel(*refs, attend, final):
    if attend:
        x_ref, yt_ref, c_ref, mk_ref, mv_ref, wo_ref, g_ref, wu_ref, wd_ref, gf_ref, o_ref = refs
    else:
        x_ref, yt_ref, c_ref, wo_ref, g_ref, wu_ref, wd_ref, gf_ref, o_ref = refs
    x = x_ref[0]
    tm = x.shape[0]
    if attend:
        cq = c_ref[0]
        mk = mk_ref[0].astype(BF16)
        mv = mv_ref[0].astype(BF16)
        lane_head = lax.broadcasted_iota(jnp.int32, (tm, MEM_WIDTH), 1) // HEAD_DIM
        cross = jnp.zeros((tm, MEM_WIDTH), F32)
        for hd in range(MEM_HEADS):
            qh = jnp.where(lane_head == hd, cq, jnp.zeros_like(cq))
            s = _dot(qh, mk) * SCALE
            e = jnp.exp(s - jnp.max(s, axis=-1, keepdims=True))
            p = e * (1.0 / jnp.sum(e, axis=-1, keepdims=True))
            cross = jnp.where(lane_head == hd, _dot_nt(p.astype(BF16), mv), cross)
        cross = cross.astype(BF16)
    else:
        cross = c_ref[0]
    y = _dot(yt_ref[0], wo_ref[:D_TOK, :]) + _dot(cross, wo_ref[D_TOK:, :])
    x1 = x + y
    hm = _rmsnorm(x1, g_ref[...]).astype(BF16)
    acc = x1
    d_ff = wu_ref.shape[1]
    for c in range(d_ff // FF_CHUNK):
        up = _dot(hm, wu_ref[:, c * FF_CHUNK:(c + 1) * FF_CHUNK])
        act = jnp.square(jnp.maximum(up, 0.0)).astype(BF16)
        acc = acc + _dot(act, wd_ref[c * FF_CHUNK:(c + 1) * FF_CHUNK, :])
    if final:
        acc = _rmsnorm(acc, gf_ref[...])
    o_ref[0] = acc


def _finish(x, yt, c, mk, mv, wo, g, wu, wd, gf, tm, final):
    b, s, d = x.shape
    attend = mk is not None
    tok = lambda wdt: pl.BlockSpec((1, tm, wdt), lambda i, j: (i, j, 0))
    ins = [x, yt, c]
    in_specs = [tok(d), tok(D_TOK), tok(MEM_WIDTH)]
    if attend:
        ins += [mk, mv]
        in_specs += [pl.BlockSpec((1,) + mk.shape[1:], lambda i, j: (i, 0, 0))] * 2
    consts = (wo, g, wu, wd, gf)
    ins += list(consts)
    in_specs += [_const_spec(cn.shape) for cn in consts]
    return pl.pallas_call(
        functools.partial(_finish_kernel, attend=attend, final=final),
        grid=(b, s // tm),
        in_specs=in_specs,
        out_specs=tok(d),
        out_shape=jax.ShapeDtypeStruct((b, s, d), F32),
        compiler_params=_params(2),
        name="finish_prompt" if attend else "finish_sample",
    )(*ins)


def _pad_rows(a, rows):
    return jnp.pad(a, ((0, 0), (0, rows - a.shape[1]), (0, 0)))


def _block_diag(w):
    per = GATE_BLOCK // HEAD_DIM
    w = w.reshape(D_TOK // GATE_BLOCK, per, HEAD_DIM, HEAD_DIM)
    eye = jnp.eye(per, dtype=w.dtype)
    out = jnp.einsum('gpij,pq->gpiqj', w, eye)
    return out.reshape(D_TOK // GATE_BLOCK, GATE_BLOCK, GATE_BLOCK).astype(BF16)


def kernel(x_prompt, x_sample, mem_prompt, cache_fox_k, cache_fox_v, cache_fox_logf, state_lru_h, state_lru_conv, cache_mem_k, cache_mem_v, page_table, norm_mix, norm_mem, norm_mlp, norm_final, w_in_fox, b_forget, w_in_lru, conv_w, conv_b, w_rgate, b_rgate, w_igate, b_igate, lru_lambda, w_mem_kv, w_out, w_up, w_down):
    b, s, d = x_prompt.shape
    db, t_new, _ = x_sample.shape
    depth = norm_mix.shape[0]
    n_pages = page_table.shape[1]
    n_mem = mem_prompt.shape[1]
    tm = min(ROW_TILE, s)
    n_s = t_new * db
    assert s % tm == 0 and tm % LANES == 0 and t_new <= SUBLANES and db % SUBLANES == 0

    row = lambda v: v.reshape(1, -1)
    xp = x_prompt
    xs = jnp.transpose(x_sample, (1, 0, 2)).reshape(n_s, d)
    ck = jnp.transpose(cache_fox_k, (0, 1, 3, 4, 2))
    cv = jnp.transpose(cache_fox_v, (0, 1, 3, 4, 2))
    clf = jnp.pad(jnp.transpose(cache_fox_logf, (0, 1, 3, 2)),
                  ((0, 0), (0, 0), (0, HEADS_PAD - FOX_HEADS), (0, 0)))
    cmk = jnp.transpose(cache_mem_k, (0, 1, 3, 4, 2)).reshape(depth, db, MEM_WIDTH, n_mem)
    cmv = jnp.transpose(cache_mem_v, (0, 1, 3, 4, 2)).reshape(depth, db, MEM_WIDTH, n_mem)
    pt_flat = page_table.reshape(-1)

    mk_all, mv_all = _mem_kv(mem_prompt, norm_mem, jnp.transpose(w_mem_kv, (0, 2, 1)).astype(BF16))

    def to_seq(a):
        return _pad_rows(jnp.transpose(a.reshape(t_new, db, -1), (1, 0, 2)), SUBLANES)

    def from_seq(a):
        return jnp.transpose(a[:, :t_new], (1, 0, 2)).reshape(n_s, -1).astype(BF16)

    pk, pv, pf, sk, sv, sf = [], [], [], [], [], []
    ph, pc, sh, sc = [], [], [], []
    for l in range(depth):
        j = l // 2
        g_mix = row(norm_mix[l])
        if l % 2 == 0:
            w = w_in_fox[j]
            wq = w[:, :D_TOK].astype(BF16)
            wkv = w[:, D_TOK:3 * D_TOK].astype(BF16)
            wkvt = jnp.transpose(w[:, D_TOK:3 * D_TOK]).astype(BF16)
            wft = jnp.pad(jnp.transpose(w[:, 3 * D_TOK:3 * D_TOK + FOX_HEADS]),
                          ((0, HEADS_PAD - FOX_HEADS), (0, 0))).astype(BF16)
            bf = jnp.pad(b_forget[j], (0, HEADS_PAD - FOX_HEADS)).reshape(HEADS_PAD, 1)
            wcq = w[:, 3 * D_TOK + FOX_HEADS:].astype(BF16)

            q_p, kt_p, vt_p, lft_p, ct_p, cq_p = _fox_in_prompt(xp, g_mix, wq, wkvt, wft, bf, wcq, tm)
            yt_p = _fox_flash(q_p, kt_p, vt_p, ct_p, tm)
            pk.append(jnp.transpose(kt_p.reshape(b, FOX_HEADS, HEAD_DIM, s), (0, 3, 1, 2)))
            pv.append(jnp.transpose(vt_p.reshape(b, FOX_HEADS, HEAD_DIM, s), (0, 3, 1, 2)))
            pf.append(jnp.transpose(lft_p[:, :FOX_HEADS], (0, 2, 1)))

            q_s, kn_s, vn_s, kt_s, vt_s, lft_s, cq_s = _fox_in_sample(
                xs, g_mix, wq, wkv, wkvt, wft, bf, wcq, t_new, db)
            rnd = lambda a: a.astype(BF16).astype(F32)
            lfn = jnp.pad(jnp.transpose(lft_s.reshape(HEADS_PAD, t_new, db), (2, 0, 1)),
                          ((0, 0), (0, 0), (0, SUBLANES - t_new)))
            yt_s8 = _fox_decode(pt_flat, to_seq(rnd(q_s)), to_seq(rnd(kn_s)), to_seq(rnd(vn_s)), lfn,
                                ck, cv, clf, j, n_pages, t_new)
            yt_s = from_seq(yt_s8)
            sk.append(jnp.transpose(kt_s.reshape(t_new, FOX_HEADS, HEAD_DIM, db), (3, 0, 1, 2)))
            sv.append(jnp.transpose(vt_s.reshape(t_new, FOX_HEADS, HEAD_DIM, db), (3, 0, 1, 2)))
            sf.append(jnp.transpose(lft_s[:FOX_HEADS].reshape(FOX_HEADS, t_new, db), (2, 1, 0)))
        else:
            consts = (g_mix, w_in_lru[j].astype(BF16), conv_w[j], row(conv_b[j]),
                      _block_diag(w_rgate[j]), row(b_rgate[j]),
                      _block_diag(w_igate[j]), row(b_igate[j]), row(lru_lambda[j]))
            yt_p, cq_p, hl_p, cs_p = _lru_prompt(xp, *consts, tm)
            ph.append(hl_p.reshape(b, D_TOK))
            pc.append(cs_p)
            yt_s, cq_s, hl_s, nb_s = _lru_sample(xs, *consts, state_lru_h[j],
                                                 jnp.transpose(state_lru_conv[j], (1, 0, 2)), t_new, db)
            yt_s = yt_s.astype(BF16)
            sh.append(hl_s)
            sc.append(jnp.transpose(nb_s, (1, 0, 2)))

        final = l == depth - 1
        wo = w_out[l].astype(BF16)
        wu = w_up[l].astype(BF16)
        wd = w_down[l].astype(BF16)
        g_mlp = row(norm_mlp[l])
        g_fin = row(norm_final)
        xp = _finish(xp, yt_p, cq_p, mk_all[l], mv_all[l], wo, g_mlp, wu, wd, g_fin, tm, final)
        cross_s = from_seq(_cross_sample(to_seq(cq_s.astype(BF16).astype(F32)), cmk, cmv, l, t_new))
        xs = _finish(xs.reshape(1, n_s, d), yt_s.reshape(1, n_s, D_TOK), cross_s.reshape(1, n_s, MEM_WIDTH),
                     None, None, wo, g_mlp, wu, wd, g_fin, n_s, final).reshape(n_s, d)

    y_sample = jnp.transpose(xs.reshape(t_new, db, d), (1, 0, 2))
    mem_out = lambda m: jnp.transpose(m.reshape(depth, b, MEM_HEADS, HEAD_DIM, n_mem), (0, 1, 4, 2, 3))
    return (xp, y_sample,
            jnp.stack(pk), jnp.stack(pv), jnp.stack(pf),
            jnp.stack(sk), jnp.stack(sv), jnp.stack(sf),
            jnp.stack(ph), jnp.stack(pc), jnp.stack(sh), jnp.stack(sc),
            mem_out(mk_all), mem_out(mv_all))
```

```python
import functools

import jax
import jax.numpy as jnp
from jax import lax
from jax.experimental import pallas as pl
from jax.experimental.pallas import tpu as pltpu

F32 = jnp.float32
BF16 = jnp.bfloat16

HEAD_DIM = 64
D_TOK = 768
MEM_WIDTH = 256
FOX_HEADS = D_TOK // HEAD_DIM
MEM_HEADS = MEM_WIDTH // HEAD_DIM
HEADS_PAD = 16
CONV_WIDTH = 4
LRU_C = 8.0
GATE_BLOCK = 256
EPS = 1e-6
SCALE = HEAD_DIM ** -0.5
NEG = -1e30
LANES = 128
SUBLANES = 8
ROW_TILE = 512
FF_CHUNK = 1024
FLASH_TQ = 512
FLASH_CK = 256
VMEM_LIMIT = 56 * 1024 * 1024


def _dot(a, b):
    return jnp.dot(a, b, preferred_element_type=F32)


def _dot_nt(a, b):
    return lax.dot_general(a, b, (((1,), (1,)), ((), ())), preferred_element_type=F32)


def _rmsnorm(x, g):
    return x * lax.rsqrt(jnp.mean(x * x, axis=-1, keepdims=True) + EPS) * g


def _softplus(x):
    return jnp.maximum(x, 0.0) + jnp.log1p(jnp.exp(-jnp.abs(x)))


def _log_sigmoid(x):
    return -_softplus(-x)


def _gelu_tanh(x):
    return x * (0.5 * (1.0 + jnp.tanh(0.7978845608028654 * (x + 0.044715 * (x * x * x)))))


def _tri_ones():
    r = lax.broadcasted_iota(jnp.int32, (LANES, 2 * LANES), 0)
    c = lax.broadcasted_iota(jnp.int32, (LANES, 2 * LANES), 1)
    return jnp.where((c >= LANES) | (r <= c), 1.0, 0.0).astype(BF16)


def _dot_f32_lhs(x, m):
    hi = x.astype(BF16)
    r1 = x - hi.astype(F32)
    mid = r1.astype(BF16)
    lo = (r1 - mid.astype(F32)).astype(BF16)
    return _dot(hi, m) + _dot(mid, m) + _dot(lo, m)


def _const_spec(shape):
    nd = len(shape)
    return pl.BlockSpec(shape, lambda *_: (0,) * nd, pipeline_mode=pl.Buffered(1))


def _params(n_axes):
    return pltpu.CompilerParams(dimension_semantics=("arbitrary",) * n_axes,
                                vmem_limit_bytes=VMEM_LIMIT)


def _memkv_kernel(m_ref, g_ref, w_ref, k_ref, v_ref):
    hm = _rmsnorm(m_ref[0], g_ref[0]).astype(BF16)
    kvt = _dot_nt(w_ref[0], hm)
    k_ref[0, 0] = kvt[:MEM_WIDTH]
    v_ref[0, 0] = kvt[MEM_WIDTH:]


def _mem_kv(mem, norm_mem, w_kv_t):
    depth = norm_mem.shape[0]
    b, n_mem, d = mem.shape
    out = jax.ShapeDtypeStruct((depth, b, MEM_WIDTH, n_mem), F32)
    return pl.pallas_call(
        _memkv_kernel,
        grid=(depth, b),
        in_specs=[pl.BlockSpec((1, n_mem, d), lambda l, i: (i, 0, 0)),
                  pl.BlockSpec((1, 1, d), lambda l, i: (l, 0, 0)),
                  pl.BlockSpec((1, 2 * MEM_WIDTH, d), lambda l, i: (l, 0, 0))],
        out_specs=[pl.BlockSpec((1, 1, MEM_WIDTH, n_mem), lambda l, i: (l, i, 0, 0))] * 2,
        out_shape=[out, out],
        compiler_params=_params(2),
        name="mem_kv",
    )(mem, norm_mem.reshape(depth, 1, d), w_kv_t)


def _split_bf16(x):
    hi = x.astype(BF16)
    r1 = x - hi.astype(F32)
    mid = r1.astype(BF16)
    lo = (r1 - mid.astype(F32)).astype(BF16)
    return hi, mid, lo


def _fox_in_prompt_kernel(*refs, tm, aliased):
    if aliased:
        refs = refs[2:]
    (x_ref, g_ref, wt_ref, wn_ref, wft_ref, bft_ref, bfn_ref,
     qt_ref, kt_ref, vt_ref, kn_ref, lft_ref, cb_ref, cq_ref, carry_ref) = refs

    @pl.when(pl.program_id(1) == 0)
    def _():
        carry_ref[...] = jnp.zeros_like(carry_ref)

    h = _rmsnorm(x_ref[0], g_ref[...]).astype(BF16)
    qkvt = _dot_nt(wt_ref[...], h)
    qt_ref[0] = (qkvt[:D_TOK] * SCALE).astype(BF16)
    kt_ref[0, 0] = qkvt[D_TOK:2 * D_TOK]
    vt_ref[0, 0] = qkvt[2 * D_TOK:]
    nat = _dot(h, wn_ref[...])
    kn_ref[0] = nat[:, :D_TOK].astype(BF16)
    cq_ref[0] = nat[:, D_TOK:D_TOK + MEM_WIDTH].astype(BF16)
    lft_ref[0] = _log_sigmoid(_dot_nt(wft_ref[...], h) + bft_ref[...])
    lane = lax.broadcasted_iota(jnp.int32, (tm, LANES), 1)
    lf = jnp.where(lane < FOX_HEADS, _log_sigmoid(nat[:, D_TOK + MEM_WIDTH:] + bfn_ref[...]), 0.0)
    r = lax.broadcasted_iota(jnp.int32, (tm, tm), 0)
    c = lax.broadcasted_iota(jnp.int32, (tm, tm), 1)
    tri = jnp.where(c <= r, 1.0, 0.0).astype(BF16)
    hi, mid, lo = _split_bf16(lf)
    cum = _dot(tri, hi) + _dot(tri, mid) + _dot(tri, lo) + carry_ref[...]
    carry_ref[...] = cum[tm - 1:tm, :]
    hi, mid, lo = _split_bf16(cum)
    packed = (hi.astype(F32) + pltpu.roll(mid.astype(F32), HEADS_PAD, axis=1)
              + pltpu.roll(lo.astype(F32), 2 * HEADS_PAD, axis=1))
    cb_ref[0] = packed.astype(BF16)


def _fox_in_prompt(x, g, wt, wn, wft, bft, bfn, tm, layer, n_fox, kt_prev, vt_prev):
    b, s, d = x.shape
    aliased = kt_prev is not None
    tok = lambda w: pl.BlockSpec((1, tm, w), lambda i, j: (i, j, 0))
    tr = lambda r: pl.BlockSpec((1, r, tm), lambda i, j: (i, 0, j))
    stacked = pl.BlockSpec((1, 1, D_TOK, tm), lambda i, j: (layer, i, 0, j))
    consts = (g, wt, wn, wft, bft, bfn)
    ins = ([kt_prev, vt_prev] if aliased else []) + [x] + list(consts)
    in_specs = ([pl.BlockSpec(memory_space=pl.ANY)] * 2 if aliased else []) \
        + [tok(d)] + [_const_spec(cn.shape) for cn in consts]
    kv_shape = jax.ShapeDtypeStruct((n_fox, b, D_TOK, s), F32)
    return pl.pallas_call(
        functools.partial(_fox_in_prompt_kernel, tm=tm, aliased=aliased),
        grid=(b, s // tm),
        in_specs=in_specs,
        out_specs=[tr(D_TOK), stacked, stacked, tok(D_TOK), tr(HEADS_PAD), tok(LANES), tok(MEM_WIDTH)],
        out_shape=[jax.ShapeDtypeStruct((b, D_TOK, s), BF16),
                   kv_shape, kv_shape,
                   jax.ShapeDtypeStruct((b, s, D_TOK), BF16),
                   jax.ShapeDtypeStruct((b, HEADS_PAD, s), F32),
                   jax.ShapeDtypeStruct((b, s, LANES), BF16),
                   jax.ShapeDtypeStruct((b, s, MEM_WIDTH), BF16)],
        scratch_shapes=[pltpu.VMEM((1, LANES), F32)],
        input_output_aliases={0: 1, 1: 2} if aliased else {},
        compiler_params=_params(2),
        name="fox_in_prompt",
    )(*ins)


def _fox_in_sample_kernel(x_ref, g_ref, wt_ref, wn_ref, wft_ref, bft_ref,
                          q_ref, kn_ref, vn_ref, kt_ref, vt_ref, lft_ref, cq_ref, *, t_new, db):
    h = _rmsnorm(x_ref[...], g_ref[...]).astype(BF16)
    qkvt = _dot_nt(wt_ref[...], h)
    q_ref[...] = jnp.transpose(qkvt[:D_TOK])
    vn_ref[...] = jnp.transpose(qkvt[2 * D_TOK:])
    for t in range(t_new):
        kt_ref[t] = qkvt[D_TOK:2 * D_TOK, t * db:(t + 1) * db]
        vt_ref[t] = qkvt[2 * D_TOK:, t * db:(t + 1) * db]
    nat = _dot(h, wn_ref[...])
    kn_ref[...] = nat[:, :D_TOK]
    cq_ref[...] = nat[:, D_TOK:D_TOK + MEM_WIDTH]
    lft_ref[...] = _log_sigmoid(_dot_nt(wft_ref[...], h) + bft_ref[...])


def _fox_in_sample(x, g, wt, wn, wft, bft, t_new, db):
    n, d = x.shape
    ins = (x, g, wt, wn, wft, bft)
    outs = [jax.ShapeDtypeStruct((n, D_TOK), F32),
            jax.ShapeDtypeStruct((n, D_TOK), F32),
            jax.ShapeDtypeStruct((n, D_TOK), F32),
            jax.ShapeDtypeStruct((t_new, D_TOK, db), F32),
            jax.ShapeDtypeStruct((t_new, D_TOK, db), F32),
            jax.ShapeDtypeStruct((HEADS_PAD, n), F32),
            jax.ShapeDtypeStruct((n, MEM_WIDTH), F32)]
    full = lambda a: pl.BlockSpec(a.shape, lambda i: (0,) * len(a.shape))
    return pl.pallas_call(
        functools.partial(_fox_in_sample_kernel, t_new=t_new, db=db),
        grid=(1,),
        in_specs=[full(a) for a in ins],
        out_specs=[full(o) for o in outs],
        out_shape=outs,
        compiler_params=_params(1),
        name="fox_in_sample",
    )(*ins)


V_ROWS = HEAD_DIM + 2 * SUBLANES


def _fox_flash_kernel(qt_ref, kn_ref, cb_ref, vt_ref, o_ref, va_ref, sa_ref, sb_ref, *, tq, ck):
    hp = pl.program_id(1)
    qi = pl.program_id(2)
    s_len = va_ref.shape[-1]

    @pl.when(qi == 0)
    def _():
        ones_row = lax.broadcasted_iota(jnp.int32, (V_ROWS - HEAD_DIM, s_len), 0) == 0
        for hh in range(2):
            va_ref[hh, :HEAD_DIM, :] = vt_ref[0, 0, hh * HEAD_DIM:(hh + 1) * HEAD_DIM, :].astype(BF16)
            va_ref[hh, HEAD_DIM:, :] = jnp.where(ones_row, 1.0, 0.0).astype(BF16)

    qt = qt_ref[0]
    sub = lax.broadcasted_iota(jnp.int32, (2 * HEAD_DIM, tq), 0)
    row = lax.broadcasted_iota(jnp.int32, (ck, tq), 0)
    col = lax.broadcasted_iota(jnp.int32, (ck, tq), 1)
    qas = []
    for hh in range(2):
        head = 2 * hp + hh
        in_head = (sub >= HEAD_DIM) if hh else (sub < HEAD_DIM)
        q_top = jnp.where(in_head, qt, jnp.zeros_like(qt))
        pick = (sub == head) | (sub == head + HEADS_PAD) | (sub == head + 2 * HEADS_PAD)
        q_bot = jnp.where(pick, -1.0, 0.0).astype(BF16)
        qas.append(jnp.concatenate([q_top, q_bot], axis=0))
    qa = jnp.concatenate(qas, axis=1)

    def logits(j):
        off = pl.multiple_of(j * ck, ck)
        ka = jnp.concatenate([kn_ref[0, pl.ds(off, ck), :], cb_ref[0, pl.ds(off, ck), :]], axis=1)
        return _dot(ka, qa)

    def update(j, st_ref, carry, diag):
        off = pl.multiple_of(j * ck, ck)
        out = []
        for hh in range(2):
            m, acc = carry[2 * hh], carry[2 * hh + 1]
            st = st_ref[:, hh * tq:(hh + 1) * tq]
            if diag is not None:
                st = jnp.where(row + diag * ck <= col, st, NEG)
            m_new = jnp.maximum(m, jnp.max(st, axis=0, keepdims=True))
            alpha = jnp.exp(m - m_new)
            p = jnp.exp(st - m_new).astype(BF16)
            acc = alpha * acc + _dot(va_ref[hh, :, pl.ds(off, ck)], p)
            out += [m_new, acc]
        return tuple(out)

    def body(i, carry):
        sb_ref[...] = logits(2 * i + 1)
        carry = update(2 * i, sa_ref, carry, None)
        sa_ref[...] = logits(2 * i + 2)
        return update(2 * i + 1, sb_ref, carry, None)

    m0 = jnp.full((1, tq), NEG, F32)
    a0 = jnp.zeros((V_ROWS, tq), F32)
    sa_ref[...] = logits(0)
    carry = lax.fori_loop(0, qi, body, (m0, a0, m0, a0))
    sb_ref[...] = logits(2 * qi + 1)
    carry = update(2 * qi, sa_ref, carry, 0)
    carry = update(2 * qi + 1, sb_ref, carry, 1)
    outs = [carry[2 * hh + 1][:HEAD_DIM] * (1.0 / carry[2 * hh + 1][HEAD_DIM:HEAD_DIM + 1])
            for hh in range(2)]
    o_ref[0] = jnp.transpose(jnp.concatenate(outs, axis=0)).astype(BF16)


def _fox_flash(qt, kn, cb, vt_all, layer, tq, ck):
    b, s, _ = kn.shape
    hw = 2 * HEAD_DIM
    return pl.pallas_call(
        functools.partial(_fox_flash_kernel, tq=tq, ck=ck),
        grid=(b, D_TOK // hw, s // tq),
        in_specs=[pl.BlockSpec((1, hw, tq), lambda i, h, j: (i, h, j)),
                  pl.BlockSpec((1, s, hw), lambda i, h, j: (i, 0, h)),
                  pl.BlockSpec((1, s, LANES), lambda i, h, j: (i, 0, 0)),
                  pl.BlockSpec((1, 1, hw, s), lambda i, h, j: (layer, i, h, 0))],
        out_specs=pl.BlockSpec((1, tq, hw), lambda i, h, j: (i, j, h)),
        out_shape=jax.ShapeDtypeStruct((b, s, D_TOK), BF16),
        scratch_shapes=[pltpu.VMEM((2, V_ROWS, s), BF16),
                        pltpu.VMEM((ck, 2 * tq), F32), pltpu.VMEM((ck, 2 * tq), F32)],
        compiler_params=_params(3),
        name="fox_flash",
    )(qt, kn, cb, vt_all)


def _fox_decode_kernel(pt_ref, q_ref, kn_ref, vn_ref, lfn_ref, *refs, n_pages, t_new):
    del pt_ref
    kp = refs[:n_pages]
    vp = refs[n_pages:2 * n_pages]
    lp = refs[2 * n_pages:3 * n_pages]
    o_ref = refs[3 * n_pages]
    rows = t_new * HEADS_PAD

    q8 = q_ref[0]
    hrow = lax.broadcasted_iota(jnp.int32, (HEADS_PAD, D_TOK), 0)
    hcol = lax.broadcasted_iota(jnp.int32, (HEADS_PAD, D_TOK), 1) // HEAD_DIM
    head_mask = jnp.where(hrow == hcol, 1.0, 0.0)
    qs = jnp.concatenate([jnp.broadcast_to(q8[t:t + 1, :], (HEADS_PAD, D_TOK)) * head_mask
                          for t in range(t_new)], axis=0)
    qsb = qs.astype(BF16)

    x = jnp.concatenate([lp[p][0, 0] for p in range(n_pages)], axis=0)
    r = _dot_f32_lhs(x, _tri_ones())
    offs = jnp.zeros((HEADS_PAD, LANES), F32)
    s_pages = []
    for p in range(n_pages):
        c_page = r[p * HEADS_PAD:(p + 1) * HEADS_PAD, :LANES] + offs
        offs = offs + r[p * HEADS_PAD:(p + 1) * HEADS_PAD, LANES:]
        kpg = kp[p][0, 0].reshape(D_TOK, LANES).astype(BF16)
        s = _dot(qsb, kpg) * SCALE
        s_pages.append(s - jnp.concatenate([c_page] * t_new, axis=0))

    kn = kn_ref[0]
    vn = vn_ref[0]
    lfn = lfn_ref[0]
    trow = lax.broadcasted_iota(jnp.int32, (rows, 1), 0) // HEADS_PAD
    c_new = offs[:, :1]
    s_new = []
    for t in range(t_new):
        c_new = c_new + lfn[:, t:t + 1]
        st = jnp.sum(qs * kn[t:t + 1, :], axis=-1, keepdims=True) * SCALE
        st = st - jnp.concatenate([c_new] * t_new, axis=0)
        s_new.append(jnp.where(trow >= t, st, NEG))

    mx = s_pages[0]
    for p in range(1, n_pages):
        mx = jnp.maximum(mx, s_pages[p])
    m = jnp.max(mx, axis=-1, keepdims=True)
    for t in range(t_new):
        m = jnp.maximum(m, s_new[t])

    lsum = jnp.zeros((rows, LANES), F32)
    o = jnp.zeros((rows, D_TOK), F32)
    for p in range(n_pages):
        pp = jnp.exp(s_pages[p] - m)
        lsum = lsum + pp
        vpg = vp[p][0, 0].reshape(D_TOK, LANES).astype(BF16)
        o = o + _dot_nt(pp.astype(BF16), vpg)
    l = jnp.sum(lsum, axis=-1, keepdims=True)
    for t in range(t_new):
        pn = jnp.exp(s_new[t] - m)
        l = l + pn
        o = o + pn.astype(BF16).astype(F32) * vn[t:t + 1, :]
    o = o * (1.0 / l)
    outs = [jnp.sum(o[t * HEADS_PAD:(t + 1) * HEADS_PAD] * head_mask, axis=0, keepdims=True)
            for t in range(t_new)]
    outs.append(jnp.zeros((SUBLANES - t_new, D_TOK), F32))
    o_ref[0] = jnp.concatenate(outs, axis=0)


def _fox_decode(page_table_flat, q8, kn8, vn8, lfn, ck, cv, clf, layer, n_pages, t_new):
    db = q8.shape[0]
    page = ck.shape[-1]
    tok = pl.BlockSpec((1, SUBLANES, D_TOK), lambda i, pt: (i, 0, 0))
    kv_spec = lambda p: pl.BlockSpec((1, 1, FOX_HEADS, HEAD_DIM, page),
                                     lambda i, pt: (layer, pt[i * n_pages + p], 0, 0, 0))
    lf_spec = lambda p: pl.BlockSpec((1, 1, HEADS_PAD, page),
                                     lambda i, pt: (layer, pt[i * n_pages + p], 0, 0))
    grid_spec = pltpu.PrefetchScalarGridSpec(
        num_scalar_prefetch=1,
        grid=(db,),
        in_specs=[tok, tok, tok, pl.BlockSpec((1, HEADS_PAD, SUBLANES), lambda i, pt: (i, 0, 0))]
        + [kv_spec(p) for p in range(n_pages)] + [kv_spec(p) for p in range(n_pages)]
        + [lf_spec(p) for p in range(n_pages)],
        out_specs=tok,
    )
    return pl.pallas_call(
        functools.partial(_fox_decode_kernel, n_pages=n_pages, t_new=t_new),
        grid_spec=grid_spec,
        out_shape=jax.ShapeDtypeStruct((db, SUBLANES, D_TOK), F32),
        compiler_params=_params(1),
        name="fox_decode",
    )(page_table_flat, q8, kn8, vn8, lfn, *([ck] * n_pages), *([cv] * n_pages), *([clf] * n_pages))


def _lru_gates(xc, gate_refs, lam_ref):
    wr_ref, br_ref, wi_ref, bi_ref = gate_refs
    xcb = xc.astype(BF16)
    nblk = D_TOK // GATE_BLOCK
    sl = lambda i: xcb[:, i * GATE_BLOCK:(i + 1) * GATE_BLOCK]
    r_pre = jnp.concatenate([_dot(sl(i), wr_ref[i]) for i in range(nblk)], axis=1) + br_ref[...]
    i_pre = jnp.concatenate([_dot(sl(i), wi_ref[i]) for i in range(nblk)], axis=1) + bi_ref[...]
    r = jax.nn.sigmoid(r_pre)
    ig = jax.nn.sigmoid(i_pre)
    log_a = -LRU_C * r * _softplus(-lam_ref[...])
    a = jnp.exp(log_a)
    u = jnp.sqrt(-jnp.tanh(log_a) * (a * a + 1.0)) * ig * xc
    return a, u


def _lru_prompt_kernel(x_ref, g_ref, w_ref, cw_ref, cb_ref, wr_ref, br_ref, wi_ref, bi_ref, lam_ref,
                       y_ref, cq_ref, hl_ref, cs_ref, tail_ref, hc_ref, *, tm):
    @pl.when(pl.program_id(1) == 0)
    def _():
        tail_ref[...] = jnp.zeros_like(tail_ref)
        hc_ref[...] = jnp.zeros_like(hc_ref)

    h = _rmsnorm(x_ref[0], g_ref[...]).astype(BF16)
    z = _dot(h, w_ref[...])
    gate = z[:, :D_TOK]
    xb = z[:, D_TOK:2 * D_TOK]
    cq_ref[0] = z[:, 2 * D_TOK:].astype(BF16)
    ext = jnp.concatenate([tail_ref[...], xb], axis=0)
    xc = cb_ref[...] + xb * cw_ref[CONV_WIDTH - 1:CONV_WIDTH, :]
    for i in range(CONV_WIDTH - 1):
        back = CONV_WIDTH - 1 - i
        xc = xc + pltpu.roll(ext, back, axis=0)[SUBLANES:, :] * cw_ref[i:i + 1, :]
    tail_ref[...] = xb[tm - SUBLANES:, :]
    cs_ref[0] = xb[tm - (CONV_WIDTH - 1):, :]

    a, u = _lru_gates(xc, (wr_ref, br_ref, wi_ref, bi_ref), lam_ref)
    groups = tm // SUBLANES
    a3 = a.reshape(groups, SUBLANES, D_TOK)
    u3 = u.reshape(groups, SUBLANES, D_TOK)
    sub = lax.broadcasted_iota(jnp.int32, (groups, SUBLANES, D_TOK), 1)
    sh = 1
    while sh < SUBLANES:
        keep = sub >= sh
        a_sh = jnp.where(keep, pltpu.roll(a3, sh, axis=1), 1.0)
        u_sh = jnp.where(keep, pltpu.roll(u3, sh, axis=1), 0.0)
        u3 = u3 + a3 * u_sh
        a3 = a3 * a_sh
        sh *= 2
    hc = hc_ref[...]
    hs = []
    for gi in range(groups):
        hg = u3[gi] + a3[gi] * hc
        hs.append(hg)
        hc = hg[SUBLANES - 1:SUBLANES, :]
    hs = jnp.concatenate(hs, axis=0)
    hc_ref[...] = hc
    hl_ref[0] = hc
    y_ref[0] = (_gelu_tanh(gate) * hs).astype(BF16)


def _lru_prompt(x, g, w, cw, cb, wr, br, wi, bi, lam, tm):
    b, s, d = x.shape
    tok = lambda wd: pl.BlockSpec((1, tm, wd), lambda i, j: (i, j, 0))
    consts = (g, w, cw, cb, wr, br, wi, bi, lam)
    return pl.pallas_call(
        functools.partial(_lru_prompt_kernel, tm=tm),
        grid=(b, s // tm),
        in_specs=[tok(d)] + [_const_spec(c.shape) for c in consts],
        out_specs=[tok(D_TOK), tok(MEM_WIDTH),
                   pl.BlockSpec((1, 1, D_TOK), lambda i, j: (i, 0, 0)),
                   pl.BlockSpec((1, CONV_WIDTH - 1, D_TOK), lambda i, j: (i, 0, 0))],
        out_shape=[jax.ShapeDtypeStruct((b, s, D_TOK), BF16),
                   jax.ShapeDtypeStruct((b, s, MEM_WIDTH), BF16),
                   jax.ShapeDtypeStruct((b, 1, D_TOK), F32),
                   jax.ShapeDtypeStruct((b, CONV_WIDTH - 1, D_TOK), F32)],
        scratch_shapes=[pltpu.VMEM((SUBLANES, D_TOK), F32), pltpu.VMEM((1, D_TOK), F32)],
        compiler_params=_params(2),
        name="lru_prompt",
    )(x, *consts)


def _lru_sample_kernel(x_ref, g_ref, w_ref, cw_ref, cb_ref, wr_ref, br_ref, wi_ref, bi_ref, lam_ref,
                       h0_ref, buf_ref, y_ref, cq_ref, hl_ref, nb_ref, *, t_new, db):
    h = _rmsnorm(x_ref[...], g_ref[...]).astype(BF16)
    z = _dot(h, w_ref[...])
    gate = z[:, :D_TOK]
    xb = z[:, D_TOK:2 * D_TOK]
    cq_ref[...] = z[:, 2 * D_TOK:]
    xp = [buf_ref[i] for i in range(CONV_WIDTH - 1)] + [xb[t * db:(t + 1) * db] for t in range(t_new)]
    xc_t = []
    for t in range(t_new):
        acc = cb_ref[...] + xp[t] * cw_ref[0:1, :]
        for i in range(1, CONV_WIDTH):
            acc = acc + xp[t + i] * cw_ref[i:i + 1, :]
        xc_t.append(acc)
    for i in range(CONV_WIDTH - 1):
        nb_ref[i] = xp[t_new + i]
    xc = jnp.concatenate(xc_t, axis=0)
    a, u = _lru_gates(xc, (wr_ref, br_ref, wi_ref, bi_ref), lam_ref)
    hc = h0_ref[...]
    hs = []
    for t in range(t_new):
        hc = a[t * db:(t + 1) * db] * hc + u[t * db:(t + 1) * db]
        hs.append(hc)
    hl_ref[...] = hc
    y_ref[...] = _gelu_tanh(gate) * jnp.concatenate(hs, axis=0)


def _lru_sample(x, g, w, cw, cb, wr, br, wi, bi, lam, h0, buf, t_new, db):
    n, d = x.shape
    ins = (x, g, w, cw, cb, wr, br, wi, bi, lam, h0, buf)
    outs = [jax.ShapeDtypeStruct((n, D_TOK), F32),
            jax.ShapeDtypeStruct((n, MEM_WIDTH), F32),
            jax.ShapeDtypeStruct((db, D_TOK), F32),
            jax.ShapeDtypeStruct((CONV_WIDTH - 1, db, D_TOK), F32)]
    full = lambda a: pl.BlockSpec(a.shape, lambda i: (0,) * len(a.shape))
    return pl.pallas_call(
        functools.partial(_lru_sample_kernel, t_new=t_new, db=db),
        grid=(1,),
        in_specs=[full(a) for a in ins],
        out_specs=[full(o) for o in outs],
        out_shape=outs,
        compiler_params=_params(1),
        name="lru_sample",
    )(*ins)


def _cross_sample_kernel(q_ref, mk_ref, mv_ref, o_ref, *, t_new, group):
    hpad = SUBLANES
    hrow = lax.broadcasted_iota(jnp.int32, (hpad, MEM_WIDTH), 0)
    hcol = lax.broadcasted_iota(jnp.int32, (hpad, MEM_WIDTH), 1) // HEAD_DIM
    head_mask = jnp.where(hrow == hcol, 1.0, 0.0)
    for i in range(group):
        q8 = q_ref[i]
        qs = jnp.concatenate([jnp.broadcast_to(q8[t:t + 1, :], (hpad, MEM_WIDTH)) * head_mask
                              for t in range(t_new)], axis=0)
        s = _dot(qs.astype(BF16), mk_ref[0, i].astype(BF16)) * SCALE
        e = jnp.exp(s - jnp.max(s, axis=-1, keepdims=True))
        p = e * (1.0 / jnp.sum(e, axis=-1, keepdims=True))
        o = _dot_nt(p.astype(BF16), mv_ref[0, i].astype(BF16))
        outs = [jnp.sum(o[t * hpad:(t + 1) * hpad] * head_mask, axis=0, keepdims=True)
                for t in range(t_new)]
        outs.append(jnp.zeros((SUBLANES - t_new, MEM_WIDTH), F32))
        o_ref[i] = jnp.concatenate(outs, axis=0)


def _cross_sample(q8, cmk, cmv, layer, t_new):
    db = q8.shape[0]
    n_mem = cmk.shape[-1]
    group = SUBLANES
    tok = pl.BlockSpec((group, SUBLANES, MEM_WIDTH), lambda i: (i, 0, 0))
    kv = pl.BlockSpec((1, group, MEM_WIDTH, n_mem), lambda i: (layer, i, 0, 0))
    return pl.pallas_call(
        functools.partial(_cross_sample_kernel, t_new=t_new, group=group),
        grid=(db // group,),
        in_specs=[tok, kv, kv],
        out_specs=tok,
        out_shape=jax.ShapeDtypeStruct((db, SUBLANES, MEM_WIDTH), F32),
        compiler_params=_params(1),
        name="cross_sample",
    )(q8, cmk, cmv)


def _finish_kernel(*refs, attend, final):
    if attend:
        x_ref, yt_ref, c_ref, mk_ref, mv_ref, wo_ref, g_ref, wu_ref, wd_ref, gf_ref, o_ref = refs
    else:
        x_ref, yt_ref, c_ref, wo_ref, g_ref, wu_ref, wd_ref, gf_ref, o_ref = refs
    x = x_ref[0]
    tm = x.shape[0]
    if attend:
        cq = c_ref[0]
        mk = mk_ref[0].astype(BF16)
        mv = mv_ref[0].astype(BF16)
        lane_head = lax.broadcasted_iota(jnp.int32, (tm, MEM_WIDTH), 1) // HEAD_DIM
        cross = jnp.zeros((tm, MEM_WIDTH), F32)
        for hd in range(MEM_HEADS):
            qh = jnp.where(lane_head == hd, cq, jnp.zeros_like(cq))
            s = _dot(qh, mk) * SCALE
            e = jnp.exp(s - jnp.max(s, axis=-1, keepdims=True))
            p = e * (1.0 / jnp.sum(e, axis=-1, keepdims=True))
            cross = jnp.where(lane_head == hd, _dot_nt(p.astype(BF16), mv), cross)
        cross = cross.astype(BF16)
    else:
        cross = c_ref[0]
    y = _dot(yt_ref[0], wo_ref[:D_TOK, :]) + _dot(cross, wo_ref[D_TOK:, :])
    x1 = x + y
    hm = _rmsnorm(x1, g_ref[...]).astype(BF16)
    acc = x1
    d_ff = wu_ref.shape[1]
    for c in range(d_ff // FF_CHUNK):
        up = _dot(hm, wu_ref[:, c * FF_CHUNK:(c + 1) * FF_CHUNK])
        act = jnp.square(jnp.maximum(up, 0.0)).astype(BF16)
        acc = acc + _dot(act, wd_ref[c * FF_CHUNK:(c + 1) * FF_CHUNK, :])
    if final:
        acc = _rmsnorm(acc, gf_ref[...])
    o_ref[0] = acc


def _finish(x, yt, c, mk, mv, wo, g, wu, wd, gf, tm, final):
    b, s, d = x.shape
    attend = mk is not None
    tok = lambda wdt: pl.BlockSpec((1, tm, wdt), lambda i, j: (i, j, 0))
    ins = [x, yt, c]
    in_specs = [tok(d), tok(D_TOK), tok(MEM_WIDTH)]
    if attend:
        ins += [mk, mv]
        in_specs += [pl.BlockSpec((1,) + mk.shape[1:], lambda i, j: (i, 0, 0))] * 2
    consts = (wo, g, wu, wd, gf)
    ins += list(consts)
    in_specs += [_const_spec(cn.shape) for cn in consts]
    return pl.pallas_call(
        functools.partial(_finish_kernel, attend=attend, final=final),
        grid=(b, s // tm),
        in_specs=in_specs,
        out_specs=tok(d),
        out_shape=jax.ShapeDtypeStruct((b, s, d), F32),
        compiler_params=_params(2),
        name="finish_prompt" if attend else "finish_sample",
    )(*ins)


def _pad_rows(a, rows):
    return jnp.pad(a, ((0, 0), (0, rows - a.shape[1]), (0, 0)))


def _block_diag(w):
    per = GATE_BLOCK // HEAD_DIM
    w = w.reshape(D_TOK // GATE_BLOCK, per, HEAD_DIM, HEAD_DIM)
    eye = jnp.eye(per, dtype=w.dtype)
    out = jnp.einsum('gpij,pq->gpiqj', w, eye)
    return out.reshape(D_TOK // GATE_BLOCK, GATE_BLOCK, GATE_BLOCK).astype(BF16)


def kernel(x_prompt, x_sample, mem_prompt, cache_fox_k, cache_fox_v, cache_fox_logf, state_lru_h, state_lru_conv, cache_mem_k, cache_mem_v, page_table, norm_mix, norm_mem, norm_mlp, norm_final, w_in_fox, b_forget, w_in_lru, conv_w, conv_b, w_rgate, b_rgate, w_igate, b_igate, lru_lambda, w_mem_kv, w_out, w_up, w_down):
    b, s, d = x_prompt.shape
    db, t_new, _ = x_sample.shape
    depth = norm_mix.shape[0]
    n_pages = page_table.shape[1]
    n_mem = mem_prompt.shape[1]
    tm = min(ROW_TILE, s)
    n_s = t_new * db
    assert s % tm == 0 and tm % LANES == 0 and t_new <= SUBLANES and db % SUBLANES == 0
    assert s % FLASH_TQ == 0 and FLASH_TQ == 2 * FLASH_CK

    row = lambda v: v.reshape(1, -1)
    xp = x_prompt
    xs = jnp.transpose(x_sample, (1, 0, 2)).reshape(n_s, d)
    ck = jnp.transpose(cache_fox_k, (0, 1, 3, 4, 2))
    cv = jnp.transpose(cache_fox_v, (0, 1, 3, 4, 2))
    clf = jnp.pad(jnp.transpose(cache_fox_logf, (0, 1, 3, 2)),
                  ((0, 0), (0, 0), (0, HEADS_PAD - FOX_HEADS), (0, 0)))
    cmk = jnp.transpose(cache_mem_k, (0, 1, 3, 4, 2)).reshape(depth, db, MEM_WIDTH, n_mem)
    cmv = jnp.transpose(cache_mem_v, (0, 1, 3, 4, 2)).reshape(depth, db, MEM_WIDTH, n_mem)
    pt_flat = page_table.reshape(-1)

    mk_all, mv_all = _mem_kv(mem_prompt, norm_mem, jnp.transpose(w_mem_kv, (0, 2, 1)).astype(BF16))

    def to_seq(a):
        return _pad_rows(jnp.transpose(a.reshape(t_new, db, -1), (1, 0, 2)), SUBLANES)

    def from_seq(a):
        return jnp.transpose(a[:, :t_new], (1, 0, 2)).reshape(n_s, -1).astype(BF16)

    pf, sk, sv, sf = [], [], [], []
    n_fox = (depth + 1) // 2
    kt_all = vt_all = None
    ph, pc, sh, sc = [], [], [], []
    for l in range(depth):
        j = l // 2
        g_mix = row(norm_mix[l])
        if l % 2 == 0:
            wall_t = jnp.transpose(w_in_fox, (2, 0, 1))[:, j, :]
            wt = wall_t[:3 * D_TOK].astype(BF16)
            wft = jnp.pad(wall_t[3 * D_TOK:3 * D_TOK + FOX_HEADS],
                          ((0, HEADS_PAD - FOX_HEADS), (0, 0))).astype(BF16)
            w = w_in_fox[j]
            wn = jnp.concatenate(
                [w[:, D_TOK:2 * D_TOK], w[:, 3 * D_TOK + FOX_HEADS:],
                 jnp.pad(w[:, 3 * D_TOK:3 * D_TOK + FOX_HEADS], ((0, 0), (0, LANES - FOX_HEADS)))],
                axis=1).astype(BF16)
            bft = jnp.pad(b_forget[j], (0, HEADS_PAD - FOX_HEADS)).reshape(HEADS_PAD, 1)
            bfn = jnp.pad(b_forget[j], (0, LANES - FOX_HEADS)).reshape(1, LANES)

            qt_p, kt_all, vt_all, kn_p, lft_p, cb_p, cq_p = _fox_in_prompt(
                xp, g_mix, wt, wn, wft, bft, bfn, tm, j, n_fox, kt_all, vt_all)
            yt_p = _fox_flash(qt_p, kn_p, cb_p, vt_all, j, FLASH_TQ, FLASH_CK)
            pf.append(jnp.transpose(lft_p[:, :FOX_HEADS], (0, 2, 1)))

            q_s, kn_s, vn_s, kt_s, vt_s, lft_s, cq_s = _fox_in_sample(
                xs, g_mix, wt, wn, wft, bft, t_new, db)
            rnd = lambda a: a.astype(BF16).astype(F32)
            lfn = jnp.pad(jnp.transpose(lft_s.reshape(HEADS_PAD, t_new, db), (2, 0, 1)),
                          ((0, 0), (0, 0), (0, SUBLANES - t_new)))
            yt_s8 = _fox_decode(pt_flat, to_seq(rnd(q_s)), to_seq(rnd(kn_s)), to_seq(rnd(vn_s)), lfn,
                                ck, cv, clf, j, n_pages, t_new)
            yt_s = from_seq(yt_s8)
            sk.append(jnp.transpose(kt_s.reshape(t_new, FOX_HEADS, HEAD_DIM, db), (3, 0, 1, 2)))
            sv.append(jnp.transpose(vt_s.reshape(t_new, FOX_HEADS, HEAD_DIM, db), (3, 0, 1, 2)))
            sf.append(jnp.transpose(lft_s[:FOX_HEADS].reshape(FOX_HEADS, t_new, db), (2, 1, 0)))
        else:
            consts = (g_mix, w_in_lru[j].astype(BF16), conv_w[j], row(conv_b[j]),
                      _block_diag(w_rgate[j]), row(b_rgate[j]),
                      _block_diag(w_igate[j]), row(b_igate[j]), row(lru_lambda[j]))
            yt_p, cq_p, hl_p, cs_p = _lru_prompt(xp, *consts, tm)
            ph.append(hl_p.reshape(b, D_TOK))
            pc.append(cs_p)
            yt_s, cq_s, hl_s, nb_s = _lru_sample(xs, *consts, state_lru_h[j],
                                                 jnp.transpose(state_lru_conv[j], (1, 0, 2)), t_new, db)
            yt_s = yt_s.astype(BF16)
            sh.append(hl_s)
            sc.append(jnp.transpose(nb_s, (1, 0, 2)))

        final = l == depth - 1
        wo = w_out[l].astype(BF16)
        wu = w_up[l].astype(BF16)
        wd = w_down[l].astype(BF16)
        g_mlp = row(norm_mlp[l])
        g_fin = row(norm_final)
        xp = _finish(xp, yt_p, cq_p, mk_all[l], mv_all[l], wo, g_mlp, wu, wd, g_fin, tm, final)
        cross_s = from_seq(_cross_sample(to_seq(cq_s.astype(BF16).astype(F32)), cmk, cmv, l, t_new))
        xs = _finish(xs.reshape(1, n_s, d), yt_s.reshape(1, n_s, D_TOK), cross_s.reshape(1, n_s, MEM_WIDTH),
                     None, None, wo, g_mlp, wu, wd, g_fin, n_s, final).reshape(n_s, d)

    y_sample = jnp.transpose(xs.reshape(t_new, db, d), (1, 0, 2))
    mem_out = lambda m: jnp.transpose(m.reshape(depth, b, MEM_HEADS, HEAD_DIM, n_mem), (0, 1, 4, 2, 3))
    fox_out = lambda a: jnp.transpose(a.reshape(n_fox, b, FOX_HEADS, HEAD_DIM, s), (0, 1, 4, 2, 3))
    return (xp, y_sample,
            fox_out(kt_all), fox_out(vt_all), jnp.stack(pf),
            jnp.stack(sk), jnp.stack(sv), jnp.stack(sf),
            jnp.stack(ph), jnp.stack(pc), jnp.stack(sh), jnp.stack(sc),
            mem_out(mk_all), mem_out(mv_all))
```

```python
import functools

import jax
import jax.numpy as jnp
from jax import lax
from jax.experimental import pallas as pl
from jax.experimental.pallas import tpu as pltpu

F32 = jnp.float32
BF16 = jnp.bfloat16

HEAD_DIM = 64
D_TOK = 768
MEM_WIDTH = 256
FOX_HEADS = D_TOK // HEAD_DIM
MEM_HEADS = MEM_WIDTH // HEAD_DIM
HEADS_PAD = 16
FOX_ROWS = 3 * D_TOK + MEM_WIDTH + HEADS_PAD
CONV_WIDTH = 4
LRU_C = 8.0
GATE_BLOCK = 256
EPS = 1e-6
SCALE = HEAD_DIM ** -0.5
LOG2E = 1.4426950408889634
NEG = -1e30
LANES = 128
SUBLANES = 8
ROW_TILE = 512
FF_CHUNK = 1024
FLASH_TQ = 1024
FLASH_CK = 256
VMEM_LIMIT = 56 * 1024 * 1024


def _dot(a, b):
    return jnp.dot(a, b, preferred_element_type=F32)


def _dot_nt(a, b):
    return lax.dot_general(a, b, (((1,), (1,)), ((), ())), preferred_element_type=F32)


def _rmsnorm(x, g):
    return x * lax.rsqrt(jnp.mean(x * x, axis=-1, keepdims=True) + EPS) * g


def _softplus(x):
    return jnp.maximum(x, 0.0) + jnp.log1p(jnp.exp(-jnp.abs(x)))


def _log_sigmoid(x):
    return -_softplus(-x)


def _gelu_tanh(x):
    return x * (0.5 * (1.0 + jnp.tanh(0.7978845608028654 * (x + 0.044715 * (x * x * x)))))


def _tri_ones():
    r = lax.broadcasted_iota(jnp.int32, (LANES, 2 * LANES), 0)
    c = lax.broadcasted_iota(jnp.int32, (LANES, 2 * LANES), 1)
    return jnp.where((c >= LANES) | (r <= c), 1.0, 0.0).astype(BF16)


def _split_bf16(x):
    hi = x.astype(BF16)
    r1 = x - hi.astype(F32)
    mid = r1.astype(BF16)
    lo = (r1 - mid.astype(F32)).astype(BF16)
    return hi, mid, lo


def _dot_f32_lhs(x, m):
    hi, mid, lo = _split_bf16(x)
    return _dot(hi, m) + _dot(mid, m) + _dot(lo, m)


def _const_spec(shape):
    nd = len(shape)
    return pl.BlockSpec(shape, lambda *_: (0,) * nd, pipeline_mode=pl.Buffered(1))


def _layer_spec(shape, layer):
    nd = len(shape) - 1
    return pl.BlockSpec((None,) + tuple(shape[1:]), lambda *_: (layer,) + (0,) * nd,
                        pipeline_mode=pl.Buffered(1))


def _params(n_axes):
    return pltpu.CompilerParams(dimension_semantics=("arbitrary",) * n_axes,
                                vmem_limit_bytes=VMEM_LIMIT)


def _memkv_kernel(m_ref, g_ref, w_ref, k_ref, v_ref):
    hm = _rmsnorm(m_ref[0], g_ref[0]).astype(BF16)
    kvt = _dot_nt(w_ref[0], hm)
    k_ref[0, 0] = kvt[:MEM_WIDTH]
    v_ref[0, 0] = kvt[MEM_WIDTH:]


def _mem_kv(mem, norm_mem, w_kv_t):
    depth = norm_mem.shape[0]
    b, n_mem, d = mem.shape
    out = jax.ShapeDtypeStruct((depth, b, MEM_WIDTH, n_mem), F32)
    return pl.pallas_call(
        _memkv_kernel,
        grid=(depth, b),
        in_specs=[pl.BlockSpec((1, n_mem, d), lambda l, i: (i, 0, 0)),
                  pl.BlockSpec((1, 1, d), lambda l, i: (l, 0, 0)),
                  pl.BlockSpec((1, 2 * MEM_WIDTH, d), lambda l, i: (l, 0, 0))],
        out_specs=[pl.BlockSpec((1, 1, MEM_WIDTH, n_mem), lambda l, i: (l, i, 0, 0))] * 2,
        out_shape=[out, out],
        compiler_params=_params(2),
        name="mem_kv",
    )(mem, norm_mem.reshape(depth, 1, d), w_kv_t)


def _fox_in_prompt_kernel(*refs, tm, aliased):
    if aliased:
        refs = refs[2:]
    (x_ref, g_ref, wt_ref, bft_ref,
     qt_ref, kt_ref, vt_ref, kn_ref, lft_ref, cb_ref, cq_ref, carry_ref) = refs

    @pl.when(pl.program_id(1) == 0)
    def _():
        carry_ref[...] = jnp.zeros_like(carry_ref)

    h = _rmsnorm(x_ref[0], g_ref[...]).astype(BF16)
    allt = _dot_nt(wt_ref[...], h)
    qt_ref[0] = (allt[:D_TOK] * (SCALE * LOG2E)).astype(BF16)
    kt = allt[D_TOK:2 * D_TOK]
    kt_ref[0, 0] = kt
    vt_ref[0, 0] = allt[2 * D_TOK:3 * D_TOK]
    kn_ref[0] = jnp.transpose(kt).astype(BF16)
    cq_ref[0] = jnp.transpose(allt[3 * D_TOK:3 * D_TOK + MEM_WIDTH]).astype(BF16)
    hrow = lax.broadcasted_iota(jnp.int32, (HEADS_PAD, tm), 0)
    lf = jnp.where(hrow < FOX_HEADS,
                   _log_sigmoid(allt[3 * D_TOK + MEM_WIDTH:] + bft_ref[...]), 0.0)
    lft_ref[0] = lf
    uo = _tri_ones()
    carry = carry_ref[...]
    pieces = []
    for blk in range(tm // LANES):
        r = _dot_f32_lhs(lf[:, blk * LANES:(blk + 1) * LANES], uo)
        pieces.append(carry + r[:, :LANES])
        carry = carry + r[:, LANES:]
    carry_ref[...] = carry
    ct = jnp.concatenate(pieces, axis=1)
    ct = jnp.concatenate([ct, jnp.zeros((LANES - HEADS_PAD, tm), F32)], axis=0)
    hi, mid, lo = _split_bf16(jnp.transpose(ct) * LOG2E)
    packed = (hi.astype(F32) + pltpu.roll(mid.astype(F32), HEADS_PAD, axis=1)
              + pltpu.roll(lo.astype(F32), 2 * HEADS_PAD, axis=1))
    cb_ref[0] = packed.astype(BF16)


def _fox_in_prompt(x, g, wt_all, bft, tm, layer, n_fox, kt_prev, vt_prev):
    b, s, d = x.shape
    aliased = kt_prev is not None
    tok = lambda w: pl.BlockSpec((1, tm, w), lambda i, j: (i, j, 0))
    tr = lambda r: pl.BlockSpec((1, r, tm), lambda i, j: (i, 0, j))
    stacked = pl.BlockSpec((1, 1, D_TOK, tm), lambda i, j: (layer, i, 0, j))
    ins = ([kt_prev, vt_prev] if aliased else []) + [x, g, wt_all, bft]
    in_specs = ([pl.BlockSpec(memory_space=pl.ANY)] * 2 if aliased else []) \
        + [tok(d), _const_spec(g.shape), _layer_spec(wt_all.shape, layer), _const_spec(bft.shape)]
    kv_shape = jax.ShapeDtypeStruct((n_fox, b, D_TOK, s), F32)
    return pl.pallas_call(
        functools.partial(_fox_in_prompt_kernel, tm=tm, aliased=aliased),
        grid=(b, s // tm),
        in_specs=in_specs,
        out_specs=[tr(D_TOK), stacked, stacked, tok(D_TOK), tr(HEADS_PAD), tok(LANES), tok(MEM_WIDTH)],
        out_shape=[jax.ShapeDtypeStruct((b, D_TOK, s), BF16),
                   kv_shape, kv_shape,
                   jax.ShapeDtypeStruct((b, s, D_TOK), BF16),
                   jax.ShapeDtypeStruct((b, HEADS_PAD, s), F32),
                   jax.ShapeDtypeStruct((b, s, LANES), BF16),
                   jax.ShapeDtypeStruct((b, s, MEM_WIDTH), BF16)],
        scratch_shapes=[pltpu.VMEM((HEADS_PAD, LANES), F32)],
        input_output_aliases={0: 1, 1: 2} if aliased else {},
        compiler_params=_params(2),
        name="fox_in_prompt",
    )(*ins)


def _fox_in_sample_kernel(x_ref, g_ref, wt_ref, bft_ref,
                          q_ref, kn_ref, vn_ref, kt_ref, vt_ref, lft_ref, cq_ref, *, t_new, db):
    h = _rmsnorm(x_ref[...], g_ref[...]).astype(BF16)
    allt = _dot_nt(wt_ref[...], h)
    q_ref[...] = jnp.transpose(allt[:D_TOK])
    kn_ref[...] = jnp.transpose(allt[D_TOK:2 * D_TOK])
    vn_ref[...] = jnp.transpose(allt[2 * D_TOK:3 * D_TOK])
    cq_ref[...] = jnp.transpose(allt[3 * D_TOK:3 * D_TOK + MEM_WIDTH])
    for t in range(t_new):
        kt_ref[t] = allt[D_TOK:2 * D_TOK, t * db:(t + 1) * db]
        vt_ref[t] = allt[2 * D_TOK:3 * D_TOK, t * db:(t + 1) * db]
    lft_ref[...] = _log_sigmoid(allt[3 * D_TOK + MEM_WIDTH:] + bft_ref[...])


def _fox_in_sample(x, g, wt_all, bft, layer, t_new, db):
    n, d = x.shape
    outs = [jax.ShapeDtypeStruct((n, D_TOK), F32),
            jax.ShapeDtypeStruct((n, D_TOK), F32),
            jax.ShapeDtypeStruct((n, D_TOK), F32),
            jax.ShapeDtypeStruct((t_new, D_TOK, db), F32),
            jax.ShapeDtypeStruct((t_new, D_TOK, db), F32),
            jax.ShapeDtypeStruct((HEADS_PAD, n), F32),
            jax.ShapeDtypeStruct((n, MEM_WIDTH), F32)]
    full = lambda a: pl.BlockSpec(a.shape, lambda i: (0,) * len(a.shape))
    return pl.pallas_call(
        functools.partial(_fox_in_sample_kernel, t_new=t_new, db=db),
        grid=(1,),
        in_specs=[full(x), full(g), _layer_spec(wt_all.shape, layer), full(bft)],
        out_specs=[full(o) for o in outs],
        out_shape=outs,
        compiler_params=_params(1),
        name="fox_in_sample",
    )(x, g, wt_all, bft)


V_ROWS = HEAD_DIM + 2 * SUBLANES


def _fox_flash_kernel(qt_ref, kn_ref, cb_ref, vt_ref, o_ref, va_ref, sa_ref, sb_ref, *, tq, ck):
    hp = pl.program_id(1)
    qi = pl.program_id(2)
    s_len = va_ref.shape[-1]

    @pl.when(qi == 0)
    def _():
        ones_row = lax.broadcasted_iota(jnp.int32, (V_ROWS - HEAD_DIM, s_len), 0) == 0
        for hh in range(2):
            va_ref[hh, :HEAD_DIM, :] = vt_ref[0, 0, hh * HEAD_DIM:(hh + 1) * HEAD_DIM, :].astype(BF16)
            va_ref[hh, HEAD_DIM:, :] = jnp.where(ones_row, 1.0, 0.0).astype(BF16)

    qt = qt_ref[0]
    sub = lax.broadcasted_iota(jnp.int32, (2 * HEAD_DIM, tq), 0)
    row = lax.broadcasted_iota(jnp.int32, (ck, ck), 0)
    col = lax.broadcasted_iota(jnp.int32, (ck, ck), 1)
    n_diag = tq // ck
    qas = []
    for hh in range(2):
        head = 2 * hp + hh
        in_head = (sub >= HEAD_DIM) if hh else (sub < HEAD_DIM)
        q_top = jnp.where(in_head, qt, jnp.zeros_like(qt))
        pick = (sub == head) | (sub == head + HEADS_PAD) | (sub == head + 2 * HEADS_PAD)
        q_bot = jnp.where(pick, -1.0, 0.0).astype(BF16)
        qas.append(jnp.concatenate([q_top, q_bot], axis=0))
    qa = jnp.concatenate(qas, axis=1)

    def keys(j):
        off = pl.multiple_of(j * ck, ck)
        return jnp.concatenate([kn_ref[0, pl.ds(off, ck), :], cb_ref[0, pl.ds(off, ck), :]], axis=1)

    def logits(j, st_ref, lo=0):
        ka = keys(j)
        if lo == 0:
            st_ref[...] = _dot(ka, qa)
        else:
            for hh in range(2):
                st_ref[:, hh * tq + lo:(hh + 1) * tq] = _dot(ka, qa[:, hh * tq + lo:(hh + 1) * tq])

    def update(j, st_ref, carry, diag):
        first_blk = 0 if diag is None else diag
        off = pl.multiple_of(j * ck, ck)
        out = list(carry)
        for hh in range(2):
            vh = va_ref[hh, :, pl.ds(off, ck)]
            for blk in range(first_blk, n_diag):
                idx = 2 * (hh * n_diag + blk)
                m, acc = carry[idx], carry[idx + 1]
                st = st_ref[:, hh * tq + blk * ck:hh * tq + (blk + 1) * ck]
                if diag is not None and blk == diag:
                    st = jnp.where(row <= col, st, NEG)
                m_new = jnp.maximum(m, jnp.max(st, axis=0, keepdims=True))
                alpha = jnp.exp2(m - m_new)
                p = jnp.exp2(st - m_new).astype(BF16)
                out[idx] = m_new
                out[idx + 1] = alpha * acc + _dot(vh, p)
        return tuple(out)

    def body(i, carry):
        logits(2 * i + 1, sb_ref)
        carry = update(2 * i, sa_ref, carry, None)
        logits(2 * i + 2, sa_ref)
        return update(2 * i + 1, sb_ref, carry, None)

    first = qi * n_diag
    m0 = jnp.full((1, ck), NEG, F32)
    a0 = jnp.zeros((V_ROWS, ck), F32)
    logits(0, sa_ref)
    carry = lax.fori_loop(0, first // 2, body, (m0, a0) * (2 * n_diag))
    bufs = (sa_ref, sb_ref)
    for dg in range(n_diag):
        if dg + 1 < n_diag:
            logits(first + dg + 1, bufs[(dg + 1) % 2], (dg + 1) * ck)
        carry = update(first + dg, bufs[dg % 2], carry, dg)
    accs = [jnp.concatenate([carry[2 * (hh * n_diag + blk) + 1] for blk in range(n_diag)], axis=1)
            for hh in range(2)]
    outs = [acc[:HEAD_DIM] * (1.0 / acc[HEAD_DIM:HEAD_DIM + 1]) for acc in accs]
    o_ref[0] = jnp.transpose(jnp.concatenate(outs, axis=0)).astype(BF16)


def _fox_flash(qt, kn, cb, vt_all, layer, tq, ck):
    b, s, _ = kn.shape
    hw = 2 * HEAD_DIM
    return pl.pallas_call(
        functools.partial(_fox_flash_kernel, tq=tq, ck=ck),
        grid=(b, D_TOK // hw, s // tq),
        in_specs=[pl.BlockSpec((1, hw, tq), lambda i, h, j: (i, h, j)),
                  pl.BlockSpec((1, s, hw), lambda i, h, j: (i, 0, h)),
                  pl.BlockSpec((1, s, LANES), lambda i, h, j: (i, 0, 0)),
                  pl.BlockSpec((1, 1, hw, s), lambda i, h, j: (layer, i, h, 0))],
        out_specs=pl.BlockSpec((1, tq, hw), lambda i, h, j: (i, j, h)),
        out_shape=jax.ShapeDtypeStruct((b, s, D_TOK), BF16),
        scratch_shapes=[pltpu.VMEM((2, V_ROWS, s), BF16),
                        pltpu.VMEM((ck, 2 * tq), F32), pltpu.VMEM((ck, 2 * tq), F32)],
        compiler_params=_params(3),
        name="fox_flash",
    )(qt, kn, cb, vt_all)


def _fox_decode_kernel(pt_ref, q_ref, kn_ref, vn_ref, lfn_ref, *refs, n_pages, t_new):
    del pt_ref
    kp = refs[:n_pages]
    vp = refs[n_pages:2 * n_pages]
    lp = refs[2 * n_pages:3 * n_pages]
    o_ref = refs[3 * n_pages]
    rows = t_new * HEADS_PAD

    q8 = q_ref[0]
    hrow = lax.broadcasted_iota(jnp.int32, (HEADS_PAD, D_TOK), 0)
    hcol = lax.broadcasted_iota(jnp.int32, (HEADS_PAD, D_TOK), 1) // HEAD_DIM
    head_mask = jnp.where(hrow == hcol, 1.0, 0.0)
    qs = jnp.concatenate([jnp.broadcast_to(q8[t:t + 1, :], (HEADS_PAD, D_TOK)) * head_mask
                          for t in range(t_new)], axis=0)
    qsb = qs.astype(BF16)

    x = jnp.concatenate([lp[p][0, 0] for p in range(n_pages)], axis=0)
    r = _dot_f32_lhs(x, _tri_ones())
    offs = jnp.zeros((HEADS_PAD, LANES), F32)
    s_pages = []
    for p in range(n_pages):
        c_page = r[p * HEADS_PAD:(p + 1) * HEADS_PAD, :LANES] + offs
        offs = offs + r[p * HEADS_PAD:(p + 1) * HEADS_PAD, LANES:]
        kpg = kp[p][0, 0].reshape(D_TOK, LANES).astype(BF16)
        s = _dot(qsb, kpg) * SCALE
        s_pages.append(s - jnp.concatenate([c_page] * t_new, axis=0))

    kn = kn_ref[0]
    vn = vn_ref[0]
    lfn = lfn_ref[0]
    trow = lax.broadcasted_iota(jnp.int32, (rows, 1), 0) // HEADS_PAD
    c_new = offs[:, :1]
    s_new = []
    for t in range(t_new):
        c_new = c_new + lfn[:, t:t + 1]
        st = jnp.sum(qs * kn[t:t + 1, :], axis=-1, keepdims=True) * SCALE
        st = st - jnp.concatenate([c_new] * t_new, axis=0)
        s_new.append(jnp.where(trow >= t, st, NEG))

    mx = s_pages[0]
    for p in range(1, n_pages):
        mx = jnp.maximum(mx, s_pages[p])
    m = jnp.max(mx, axis=-1, keepdims=True)
    for t in range(t_new):
        m = jnp.maximum(m, s_new[t])

    lsum = jnp.zeros((rows, LANES), F32)
    o = jnp.zeros((rows, D_TOK), F32)
    for p in range(n_pages):
        pp = jnp.exp(s_pages[p] - m)
        lsum = lsum + pp
        vpg = vp[p][0, 0].reshape(D_TOK, LANES).astype(BF16)
        o = o + _dot_nt(pp.astype(BF16), vpg)
    l = jnp.sum(lsum, axis=-1, keepdims=True)
    for t in range(t_new):
        pn = jnp.exp(s_new[t] - m)
        l = l + pn
        o = o + pn.astype(BF16).astype(F32) * vn[t:t + 1, :]
    o = o * (1.0 / l)
    outs = [jnp.sum(o[t * HEADS_PAD:(t + 1) * HEADS_PAD] * head_mask, axis=0, keepdims=True)
            for t in range(t_new)]
    outs.append(jnp.zeros((SUBLANES - t_new, D_TOK), F32))
    o_ref[0] = jnp.concatenate(outs, axis=0)


def _fox_decode(page_table_flat, q8, kn8, vn8, lfn, ck, cv, clf, layer, n_pages, t_new):
    db = q8.shape[0]
    page = ck.shape[-1]
    tok = pl.BlockSpec((1, SUBLANES, D_TOK), lambda i, pt: (i, 0, 0))
    kv_spec = lambda p: pl.BlockSpec((1, 1, FOX_HEADS, HEAD_DIM, page),
                                     lambda i, pt: (layer, pt[i * n_pages + p], 0, 0, 0))
    lf_spec = lambda p: pl.BlockSpec((1, 1, HEADS_PAD, page),
                                     lambda i, pt: (layer, pt[i * n_pages + p], 0, 0))
    grid_spec = pltpu.PrefetchScalarGridSpec(
        num_scalar_prefetch=1,
        grid=(db,),
        in_specs=[tok, tok, tok, pl.BlockSpec((1, HEADS_PAD, SUBLANES), lambda i, pt: (i, 0, 0))]
        + [kv_spec(p) for p in range(n_pages)] + [kv_spec(p) for p in range(n_pages)]
        + [lf_spec(p) for p in range(n_pages)],
        out_specs=tok,
    )
    return pl.pallas_call(
        functools.partial(_fox_decode_kernel, n_pages=n_pages, t_new=t_new),
        grid_spec=grid_spec,
        out_shape=jax.ShapeDtypeStruct((db, SUBLANES, D_TOK), F32),
        compiler_params=_params(1),
        name="fox_decode",
    )(page_table_flat, q8, kn8, vn8, lfn, *([ck] * n_pages), *([cv] * n_pages), *([clf] * n_pages))


def _lru_gates(xc, gate_refs, lam_ref):
    wr_ref, br_ref, wi_ref, bi_ref = gate_refs
    xcb = xc.astype(BF16)
    nblk = D_TOK // GATE_BLOCK
    sl = lambda i: xcb[:, i * GATE_BLOCK:(i + 1) * GATE_BLOCK]
    r_pre = jnp.concatenate([_dot(sl(i), wr_ref[i]) for i in range(nblk)], axis=1) + br_ref[...]
    i_pre = jnp.concatenate([_dot(sl(i), wi_ref[i]) for i in range(nblk)], axis=1) + bi_ref[...]
    r = jax.nn.sigmoid(r_pre)
    ig = jax.nn.sigmoid(i_pre)
    log_a = -LRU_C * r * _softplus(-lam_ref[...])
    a = jnp.exp(log_a)
    u = jnp.sqrt(-jnp.tanh(log_a) * (a * a + 1.0)) * ig * xc
    return a, u


def _lru_prompt_kernel(x_ref, g_ref, w_ref, cw_ref, cb_ref, wr_ref, br_ref, wi_ref, bi_ref, lam_ref,
                       y_ref, cq_ref, hl_ref, cs_ref, tail_ref, hc_ref, *, tm):
    @pl.when(pl.program_id(1) == 0)
    def _():
        tail_ref[...] = jnp.zeros_like(tail_ref)
        hc_ref[...] = jnp.zeros_like(hc_ref)

    h = _rmsnorm(x_ref[0], g_ref[...]).astype(BF16)
    z = _dot(h, w_ref[...])
    gate = z[:, :D_TOK]
    xb = z[:, D_TOK:2 * D_TOK]
    cq_ref[0] = z[:, 2 * D_TOK:].astype(BF16)
    ext = jnp.concatenate([tail_ref[...], xb], axis=0)
    xc = cb_ref[...] + xb * cw_ref[CONV_WIDTH - 1:CONV_WIDTH, :]
    for i in range(CONV_WIDTH - 1):
        back = CONV_WIDTH - 1 - i
        xc = xc + pltpu.roll(ext, back, axis=0)[SUBLANES:, :] * cw_ref[i:i + 1, :]
    tail_ref[...] = xb[tm - SUBLANES:, :]
    cs_ref[0] = xb[tm - (CONV_WIDTH - 1):, :]

    a, u = _lru_gates(xc, (wr_ref, br_ref, wi_ref, bi_ref), lam_ref)
    groups = tm // SUBLANES
    a3 = a.reshape(groups, SUBLANES, D_TOK)
    u3 = u.reshape(groups, SUBLANES, D_TOK)
    sub = lax.broadcasted_iota(jnp.int32, (groups, SUBLANES, D_TOK), 1)
    sh = 1
    while sh < SUBLANES:
        keep = sub >= sh
        a_sh = jnp.where(keep, pltpu.roll(a3, sh, axis=1), 1.0)
        u_sh = jnp.where(keep, pltpu.roll(u3, sh, axis=1), 0.0)
        u3 = u3 + a3 * u_sh
        a3 = a3 * a_sh
        sh *= 2
    hc = hc_ref[...]
    hs = []
    for gi in range(groups):
        hg = u3[gi] + a3[gi] * hc
        hs.append(hg)
        hc = hg[SUBLANES - 1:SUBLANES, :]
    hs = jnp.concatenate(hs, axis=0)
    hc_ref[...] = hc
    hl_ref[0] = hc
    y_ref[0] = (_gelu_tanh(gate) * hs).astype(BF16)


def _lru_prompt(x, g, w_all, cw, cb, wr, br, wi, bi, lam, layer, tm):
    b, s, d = x.shape
    tok = lambda wd: pl.BlockSpec((1, tm, wd), lambda i, j: (i, j, 0))
    consts = (g, w_all, cw, cb, wr, br, wi, bi, lam)
    specs = [_const_spec(c.shape) for c in consts]
    specs[1] = _layer_spec(w_all.shape, layer)
    return pl.pallas_call(
        functools.partial(_lru_prompt_kernel, tm=tm),
        grid=(b, s // tm),
        in_specs=[tok(d)] + specs,
        out_specs=[tok(D_TOK), tok(MEM_WIDTH),
                   pl.BlockSpec((1, 1, D_TOK), lambda i, j: (i, 0, 0)),
                   pl.BlockSpec((1, CONV_WIDTH - 1, D_TOK), lambda i, j: (i, 0, 0))],
        out_shape=[jax.ShapeDtypeStruct((b, s, D_TOK), BF16),
                   jax.ShapeDtypeStruct((b, s, MEM_WIDTH), BF16),
                   jax.ShapeDtypeStruct((b, 1, D_TOK), F32),
                   jax.ShapeDtypeStruct((b, CONV_WIDTH - 1, D_TOK), F32)],
        scratch_shapes=[pltpu.VMEM((SUBLANES, D_TOK), F32), pltpu.VMEM((1, D_TOK), F32)],
        compiler_params=_params(2),
        name="lru_prompt",
    )(x, *consts)


def _lru_sample_kernel(x_ref, g_ref, w_ref, cw_ref, cb_ref, wr_ref, br_ref, wi_ref, bi_ref, lam_ref,
                       h0_ref, buf_ref, y_ref, cq_ref, hl_ref, nb_ref, *, t_new, db):
    h = _rmsnorm(x_ref[...], g_ref[...]).astype(BF16)
    z = _dot(h, w_ref[...])
    gate = z[:, :D_TOK]
    xb = z[:, D_TOK:2 * D_TOK]
    cq_ref[...] = z[:, 2 * D_TOK:]
    xp = [buf_ref[i] for i in range(CONV_WIDTH - 1)] + [xb[t * db:(t + 1) * db] for t in range(t_new)]
    xc_t = []
    for t in range(t_new):
        acc = cb_ref[...] + xp[t] * cw_ref[0:1, :]
        for i in range(1, CONV_WIDTH):
            acc = acc + xp[t + i] * cw_ref[i:i + 1, :]
        xc_t.append(acc)
    for i in range(CONV_WIDTH - 1):
        nb_ref[i] = xp[t_new + i]
    xc = jnp.concatenate(xc_t, axis=0)
    a, u = _lru_gates(xc, (wr_ref, br_ref, wi_ref, bi_ref), lam_ref)
    hc = h0_ref[...]
    hs = []
    for t in range(t_new):
        hc = a[t * db:(t + 1) * db] * hc + u[t * db:(t + 1) * db]
        hs.append(hc)
    hl_ref[...] = hc
    y_ref[...] = _gelu_tanh(gate) * jnp.concatenate(hs, axis=0)


def _lru_sample(x, g, w_all, cw, cb, wr, br, wi, bi, lam, h0, buf, layer, t_new, db):
    n, d = x.shape
    ins = (x, g, w_all, cw, cb, wr, br, wi, bi, lam, h0, buf)
    outs = [jax.ShapeDtypeStruct((n, D_TOK), F32),
            jax.ShapeDtypeStruct((n, MEM_WIDTH), F32),
            jax.ShapeDtypeStruct((db, D_TOK), F32),
            jax.ShapeDtypeStruct((CONV_WIDTH - 1, db, D_TOK), F32)]
    full = lambda a: pl.BlockSpec(a.shape, lambda i: (0,) * len(a.shape))
    return pl.pallas_call(
        functools.partial(_lru_sample_kernel, t_new=t_new, db=db),
        grid=(1,),
        in_specs=[_layer_spec(a.shape, layer) if a is w_all else full(a) for a in ins],
        out_specs=[full(o) for o in outs],
        out_shape=outs,
        compiler_params=_params(1),
        name="lru_sample",
    )(*ins)


def _cross_sample_kernel(q_ref, mk_ref, mv_ref, o_ref, *, t_new, group):
    hpad = SUBLANES
    hrow = lax.broadcasted_iota(jnp.int32, (hpad, MEM_WIDTH), 0)
    hcol = lax.broadcasted_iota(jnp.int32, (hpad, MEM_WIDTH), 1) // HEAD_DIM
    head_mask = jnp.where(hrow == hcol, 1.0, 0.0)
    for i in range(group):
        q8 = q_ref[i]
        qs = jnp.concatenate([jnp.broadcast_to(q8[t:t + 1, :], (hpad, MEM_WIDTH)) * head_mask
                              for t in range(t_new)], axis=0)
        s = _dot(qs.astype(BF16), mk_ref[0, i].astype(BF16)) * SCALE
        e = jnp.exp(s - jnp.max(s, axis=-1, keepdims=True))
        p = e * (1.0 / jnp.sum(e, axis=-1, keepdims=True))
        o = _dot_nt(p.astype(BF16), mv_ref[0, i].astype(BF16))
        outs = [jnp.sum(o[t * hpad:(t + 1) * hpad] * head_mask, axis=0, keepdims=True)
                for t in range(t_new)]
        outs.append(jnp.zeros((SUBLANES - t_new, MEM_WIDTH), F32))
        o_ref[i] = jnp.concatenate(outs, axis=0)


def _cross_sample(q8, cmk, cmv, layer, t_new):
    db = q8.shape[0]
    n_mem = cmk.shape[-1]
    group = SUBLANES
    tok = pl.BlockSpec((group, SUBLANES, MEM_WIDTH), lambda i: (i, 0, 0))
    kv = pl.BlockSpec((1, group, MEM_WIDTH, n_mem), lambda i: (layer, i, 0, 0))
    return pl.pallas_call(
        functools.partial(_cross_sample_kernel, t_new=t_new, group=group),
        grid=(db // group,),
        in_specs=[tok, kv, kv],
        out_specs=tok,
        out_shape=jax.ShapeDtypeStruct((db, SUBLANES, MEM_WIDTH), F32),
        compiler_params=_params(1),
        name="cross_sample",
    )(q8, cmk, cmv)


def _cross_attend(cq, mk, mv):
    tm = cq.shape[0]
    mk = mk.astype(BF16)
    mv = mv.astype(BF16)
    lane_head = lax.broadcasted_iota(jnp.int32, (tm, MEM_WIDTH), 1) // HEAD_DIM
    cross = jnp.zeros((tm, MEM_WIDTH), F32)
    for hd in range(MEM_HEADS):
        qh = jnp.where(lane_head == hd, cq, jnp.zeros_like(cq))
        s = _dot(qh, mk) * SCALE
        e = jnp.exp(s - jnp.max(s, axis=-1, keepdims=True))
        p = e * (1.0 / jnp.sum(e, axis=-1, keepdims=True))
        cross = jnp.where(lane_head == hd, _dot_nt(p.astype(BF16), mv), cross)
    return cross.astype(BF16)


def _finish_kernel(*refs, attend, final):
    if attend:
        x_ref, yt_ref, c_ref, mk_ref, mv_ref, wo_ref, g_ref, wu_ref, wd_ref, gf_ref, o_ref = refs
        cross = _cross_attend(c_ref[0], mk_ref[...], mv_ref[...])
    else:
        x_ref, yt_ref, c_ref, wo_ref, g_ref, wu_ref, wd_ref, gf_ref, o_ref = refs
        cross = c_ref[0]
    y = _dot(yt_ref[0], wo_ref[:D_TOK, :]) + _dot(cross, wo_ref[D_TOK:, :])
    x1 = x_ref[0] + y
    hm = _rmsnorm(x1, g_ref[...]).astype(BF16)
    acc = x1
    d_ff = wu_ref.shape[1]
    for c in range(d_ff // FF_CHUNK):
        up = _dot(hm, wu_ref[:, c * FF_CHUNK:(c + 1) * FF_CHUNK])
        act = jnp.square(jnp.maximum(up, 0.0)).astype(BF16)
        acc = acc + _dot(act, wd_ref[c * FF_CHUNK:(c + 1) * FF_CHUNK, :])
    if final:
        acc = _rmsnorm(acc, gf_ref[...])
    o_ref[0] = acc


def _finish(x, yt, c, mk_all, mv_all, wo_all, g_all, wu_all, wd_all, gf, layer, tm, final):
    b, s, d = x.shape
    attend = mk_all is not None
    tok = lambda wdt: pl.BlockSpec((1, tm, wdt), lambda i, j: (i, j, 0))
    ins = [x, yt, c]
    in_specs = [tok(d), tok(D_TOK), tok(MEM_WIDTH)]
    if attend:
        ins += [mk_all, mv_all]
        in_specs += [pl.BlockSpec((None, None) + mk_all.shape[2:], lambda i, j: (layer, i, 0, 0))] * 2
    ins += [wo_all, g_all, wu_all, wd_all, gf]
    in_specs += [_layer_spec(wo_all.shape, layer), _layer_spec(g_all.shape, layer),
                 _layer_spec(wu_all.shape, layer), _layer_spec(wd_all.shape, layer), _const_spec(gf.shape)]
    return pl.pallas_call(
        functools.partial(_finish_kernel, attend=attend, final=final),
        grid=(b, s // tm),
        in_specs=in_specs,
        out_specs=tok(d),
        out_shape=jax.ShapeDtypeStruct((b, s, d), F32),
        compiler_params=_params(2),
        name="finish_prompt" if attend else "finish_sample",
    )(*ins)


def _pad_rows(a, rows):
    return jnp.pad(a, ((0, 0), (0, rows - a.shape[1]), (0, 0)))


def _block_diag(w):
    per = GATE_BLOCK // HEAD_DIM
    w = w.reshape(D_TOK // GATE_BLOCK, per, HEAD_DIM, HEAD_DIM)
    eye = jnp.eye(per, dtype=w.dtype)
    out = jnp.einsum('gpij,pq->gpiqj', w, eye)
    return out.reshape(D_TOK // GATE_BLOCK, GATE_BLOCK, GATE_BLOCK).astype(BF16)


def kernel(x_prompt, x_sample, mem_prompt, cache_fox_k, cache_fox_v, cache_fox_logf, state_lru_h, state_lru_conv, cache_mem_k, cache_mem_v, page_table, norm_mix, norm_mem, norm_mlp, norm_final, w_in_fox, b_forget, w_in_lru, conv_w, conv_b, w_rgate, b_rgate, w_igate, b_igate, lru_lambda, w_mem_kv, w_out, w_up, w_down):
    b, s, d = x_prompt.shape
    db, t_new, _ = x_sample.shape
    depth = norm_mix.shape[0]
    n_pages = page_table.shape[1]
    n_mem = mem_prompt.shape[1]
    tm = min(ROW_TILE, s)
    n_s = t_new * db
    assert s % tm == 0 and tm % LANES == 0 and t_new <= SUBLANES and db % SUBLANES == 0
    assert s % FLASH_TQ == 0 and FLASH_TQ % (2 * FLASH_CK) == 0

    row = lambda v: v.reshape(1, -1)
    xp = x_prompt
    xs = jnp.transpose(x_sample, (1, 0, 2)).reshape(n_s, d)
    ck = jnp.transpose(cache_fox_k, (0, 1, 3, 4, 2))
    cv = jnp.transpose(cache_fox_v, (0, 1, 3, 4, 2))
    clf = jnp.pad(jnp.transpose(cache_fox_logf, (0, 1, 3, 2)),
                  ((0, 0), (0, 0), (0, HEADS_PAD - FOX_HEADS), (0, 0)))
    cmk = jnp.transpose(cache_mem_k, (0, 1, 3, 4, 2)).reshape(depth, db, MEM_WIDTH, n_mem)
    cmv = jnp.transpose(cache_mem_v, (0, 1, 3, 4, 2)).reshape(depth, db, MEM_WIDTH, n_mem)
    pt_flat = page_table.reshape(-1)

    mk_all, mv_all = _mem_kv(mem_prompt, norm_mem, jnp.transpose(w_mem_kv, (0, 2, 1)).astype(BF16))

    wo_all = w_out.astype(BF16)
    wu_all = w_up.astype(BF16)
    wd_all = w_down.astype(BF16)
    w_lru_all = w_in_lru.astype(BF16)
    g_mlp_all = norm_mlp.reshape(depth, 1, d)
    g_fin = row(norm_final)
    wall_t = jnp.transpose(w_in_fox, (2, 0, 1))
    n_qkv = 3 * D_TOK
    w_fox_all = jnp.transpose(jnp.concatenate(
        [wall_t[:n_qkv], wall_t[n_qkv + FOX_HEADS:],
         jnp.pad(wall_t[n_qkv:n_qkv + FOX_HEADS], ((0, HEADS_PAD - FOX_HEADS), (0, 0), (0, 0)))],
        axis=0), (1, 0, 2)).astype(BF16)

    def to_seq(a):
        return _pad_rows(jnp.transpose(a.reshape(t_new, db, -1), (1, 0, 2)), SUBLANES)

    def from_seq(a):
        return jnp.transpose(a[:, :t_new], (1, 0, 2)).reshape(n_s, -1).astype(BF16)

    rnd = lambda a: a.astype(BF16).astype(F32)
    pf, sk, sv, sf = [], [], [], []
    n_fox = (depth + 1) // 2
    kt_all = vt_all = None
    ph, pc, sh, sc = [], [], [], []
    for l in range(depth):
        j = l // 2
        final = l == depth - 1
        g_mix = row(norm_mix[l])
        fin_w = (wo_all, g_mlp_all, wu_all, wd_all, g_fin, l)
        mem_l = (mk_all, mv_all)
        if l % 2 == 0:
            bft = jnp.pad(b_forget[j], (0, HEADS_PAD - FOX_HEADS)).reshape(HEADS_PAD, 1)
            qt_p, kt_all, vt_all, kn_p, lft_p, cb_p, cq_p = _fox_in_prompt(
                xp, g_mix, w_fox_all, bft, tm, j, n_fox, kt_all, vt_all)
            yt_p = _fox_flash(qt_p, kn_p, cb_p, vt_all, j, FLASH_TQ, FLASH_CK)
            pf.append(jnp.transpose(lft_p[:, :FOX_HEADS], (0, 2, 1)))

            q_s, kn_s, vn_s, kt_s, vt_s, lft_s, cq_s = _fox_in_sample(xs, g_mix, w_fox_all, bft, j, t_new, db)
            lfn = jnp.pad(jnp.transpose(lft_s.reshape(HEADS_PAD, t_new, db), (2, 0, 1)),
                          ((0, 0), (0, 0), (0, SUBLANES - t_new)))
            xp = _finish(xp, yt_p, cq_p, *mem_l, *fin_w, tm, final)
            yt_s8 = _fox_decode(pt_flat, to_seq(rnd(q_s)), to_seq(rnd(kn_s)), to_seq(rnd(vn_s)), lfn,
                                ck, cv, clf, j, n_pages, t_new)
            yt_s = from_seq(yt_s8)
            sk.append(jnp.transpose(kt_s.reshape(t_new, FOX_HEADS, HEAD_DIM, db), (3, 0, 1, 2)))
            sv.append(jnp.transpose(vt_s.reshape(t_new, FOX_HEADS, HEAD_DIM, db), (3, 0, 1, 2)))
            sf.append(jnp.transpose(lft_s[:FOX_HEADS].reshape(FOX_HEADS, t_new, db), (2, 1, 0)))
        else:
            consts = (g_mix, w_lru_all, conv_w[j], row(conv_b[j]),
                      _block_diag(w_rgate[j]), row(b_rgate[j]),
                      _block_diag(w_igate[j]), row(b_igate[j]), row(lru_lambda[j]))
            yt_p, cq_p, hl_p, cs_p = _lru_prompt(xp, *consts, j, tm)
            ph.append(hl_p.reshape(b, D_TOK))
            pc.append(cs_p)
            yt_s, cq_s, hl_s, nb_s = _lru_sample(xs, *consts, state_lru_h[j],
                                                 jnp.transpose(state_lru_conv[j], (1, 0, 2)), j, t_new, db)
            yt_s = yt_s.astype(BF16)
            sh.append(hl_s)
            sc.append(jnp.transpose(nb_s, (1, 0, 2)))
            xp = _finish(xp, yt_p, cq_p, *mem_l, *fin_w, tm, final)

        cross_s = from_seq(_cross_sample(to_seq(rnd(cq_s)), cmk, cmv, l, t_new))
        xs = _finish(xs.reshape(1, n_s, d), yt_s.reshape(1, n_s, D_TOK), cross_s.reshape(1, n_s, MEM_WIDTH),
                     None, None, *fin_w, n_s, final).reshape(n_s, d)

    y_sample = jnp.transpose(xs.reshape(t_new, db, d), (1, 0, 2))
    mem_out = lambda m: jnp.transpose(m.reshape(depth, b, MEM_HEADS, HEAD_DIM, n_mem), (0, 1, 4, 2, 3))
    fox_out = lambda a: jnp.transpose(a.reshape(n_fox, b, FOX_HEADS, HEAD_DIM, s), (0, 1, 4, 2, 3))
    return (xp, y_sample,
            fox_out(kt_all), fox_out(vt_all), jnp.stack(pf),
            jnp.stack(sk), jnp.stack(sv), jnp.stack(sf),
            jnp.stack(ph), jnp.stack(pc), jnp.stack(sh), jnp.stack(sc),
            mem_out(mk_all), mem_out(mv_all))
```

```python
import functools

import jax
import jax.numpy as jnp
from jax import lax
from jax.experimental import pallas as pl
from jax.experimental.pallas import tpu as pltpu

F32 = jnp.float32
BF16 = jnp.bfloat16

HEAD_DIM = 64
D_TOK = 768
MEM_WIDTH = 256
FOX_HEADS = D_TOK // HEAD_DIM
MEM_HEADS = MEM_WIDTH // HEAD_DIM
HEADS_PAD = 16
FOX_ROWS = 3 * D_TOK + MEM_WIDTH + HEADS_PAD
CONV_WIDTH = 4
LRU_C = 8.0
GATE_BLOCK = 256
EPS = 1e-6
SCALE = HEAD_DIM ** -0.5
LOG2E = 1.4426950408889634
NEG = -1e30
LANES = 128
SUBLANES = 8
ROW_TILE = 512
FF_CHUNK = 1024
FLASH_TQ = 2048
FLASH_CK = 256
VMEM_LIMIT = 56 * 1024 * 1024


def _dot(a, b):
    return jnp.dot(a, b, preferred_element_type=F32)


def _dot_nt(a, b):
    return lax.dot_general(a, b, (((1,), (1,)), ((), ())), preferred_element_type=F32)


def _rmsnorm(x, g):
    return x * lax.rsqrt(jnp.mean(x * x, axis=-1, keepdims=True) + EPS) * g


def _softplus(x):
    return jnp.maximum(x, 0.0) + jnp.log1p(jnp.exp(-jnp.abs(x)))


def _log_sigmoid(x):
    return -_softplus(-x)


def _gelu_tanh(x):
    return x * (0.5 * (1.0 + jnp.tanh(0.7978845608028654 * (x + 0.044715 * (x * x * x)))))


def _tri_ones():
    r = lax.broadcasted_iota(jnp.int32, (LANES, 2 * LANES), 0)
    c = lax.broadcasted_iota(jnp.int32, (LANES, 2 * LANES), 1)
    return jnp.where((c >= LANES) | (r <= c), 1.0, 0.0).astype(BF16)


def _split_bf16(x):
    hi = x.astype(BF16)
    r1 = x - hi.astype(F32)
    mid = r1.astype(BF16)
    lo = (r1 - mid.astype(F32)).astype(BF16)
    return hi, mid, lo


def _dot_f32_lhs(x, m):
    hi, mid, lo = _split_bf16(x)
    return _dot(hi, m) + _dot(mid, m) + _dot(lo, m)


def _const_spec(shape):
    nd = len(shape)
    return pl.BlockSpec(shape, lambda *_: (0,) * nd, pipeline_mode=pl.Buffered(1))


def _layer_spec(shape, layer):
    nd = len(shape) - 1
    return pl.BlockSpec((None,) + tuple(shape[1:]), lambda *_: (layer,) + (0,) * nd,
                        pipeline_mode=pl.Buffered(1))


def _params(n_axes):
    return pltpu.CompilerParams(dimension_semantics=("arbitrary",) * n_axes,
                                vmem_limit_bytes=VMEM_LIMIT)


def _memkv_kernel(m_ref, g_ref, w_ref, k_ref, v_ref):
    hm = _rmsnorm(m_ref[0], g_ref[0]).astype(BF16)
    kvt = _dot_nt(w_ref[0], hm)
    k_ref[0, 0] = kvt[:MEM_WIDTH]
    v_ref[0, 0] = kvt[MEM_WIDTH:]


def _mem_kv(mem, norm_mem, w_kv_t):
    depth = norm_mem.shape[0]
    b, n_mem, d = mem.shape
    out = jax.ShapeDtypeStruct((depth, b, MEM_WIDTH, n_mem), F32)
    return pl.pallas_call(
        _memkv_kernel,
        grid=(depth, b),
        in_specs=[pl.BlockSpec((1, n_mem, d), lambda l, i: (i, 0, 0)),
                  pl.BlockSpec((1, 1, d), lambda l, i: (l, 0, 0)),
                  pl.BlockSpec((1, 2 * MEM_WIDTH, d), lambda l, i: (l, 0, 0))],
        out_specs=[pl.BlockSpec((1, 1, MEM_WIDTH, n_mem), lambda l, i: (l, i, 0, 0))] * 2,
        out_shape=[out, out],
        compiler_params=_params(2),
        name="mem_kv",
    )(mem, norm_mem.reshape(depth, 1, d), w_kv_t)


def _fox_in_prompt_kernel(*refs, tm, aliased):
    if aliased:
        refs = refs[2:]
    (x_ref, g_ref, wt_ref, bft_ref,
     qt_ref, kt_ref, vt_ref, kn_ref, lft_ref, cb_ref, cq_ref, carry_ref) = refs

    @pl.when(pl.program_id(1) == 0)
    def _():
        carry_ref[...] = jnp.zeros_like(carry_ref)

    h = _rmsnorm(x_ref[0], g_ref[...]).astype(BF16)
    allt = _dot_nt(wt_ref[...], h)
    qt_ref[0] = (allt[:D_TOK] * (SCALE * LOG2E)).astype(BF16)
    kt = allt[D_TOK:2 * D_TOK]
    kt_ref[0, 0] = kt
    vt_ref[0, 0] = allt[2 * D_TOK:3 * D_TOK]
    kn_ref[0] = jnp.transpose(kt).astype(BF16)
    cq_ref[0] = jnp.transpose(allt[3 * D_TOK:3 * D_TOK + MEM_WIDTH]).astype(BF16)
    hrow = lax.broadcasted_iota(jnp.int32, (HEADS_PAD, tm), 0)
    lf = jnp.where(hrow < FOX_HEADS,
                   _log_sigmoid(allt[3 * D_TOK + MEM_WIDTH:] + bft_ref[...]), 0.0)
    lft_ref[0] = lf
    uo = _tri_ones()
    carry = carry_ref[...]
    pieces = []
    for blk in range(tm // LANES):
        r = _dot_f32_lhs(lf[:, blk * LANES:(blk + 1) * LANES], uo)
        pieces.append(carry + r[:, :LANES])
        carry = carry + r[:, LANES:]
    carry_ref[...] = carry
    ct = jnp.concatenate(pieces, axis=1)
    ct = jnp.concatenate([ct, jnp.zeros((LANES - HEADS_PAD, tm), F32)], axis=0)
    hi, mid, lo = _split_bf16(jnp.transpose(ct) * LOG2E)
    packed = (hi.astype(F32) + pltpu.roll(mid.astype(F32), HEADS_PAD, axis=1)
              + pltpu.roll(lo.astype(F32), 2 * HEADS_PAD, axis=1))
    cb_ref[0] = packed.astype(BF16)


def _fox_in_prompt(x, g, wt_all, bft, tm, layer, n_fox, kt_prev, vt_prev):
    b, s, d = x.shape
    aliased = kt_prev is not None
    tok = lambda w: pl.BlockSpec((1, tm, w), lambda i, j: (i, j, 0))
    tr = lambda r: pl.BlockSpec((1, r, tm), lambda i, j: (i, 0, j))
    stacked = pl.BlockSpec((1, 1, D_TOK, tm), lambda i, j: (layer, i, 0, j))
    ins = ([kt_prev, vt_prev] if aliased else []) + [x, g, wt_all, bft]
    in_specs = ([pl.BlockSpec(memory_space=pl.ANY)] * 2 if aliased else []) \
        + [tok(d), _const_spec(g.shape), _layer_spec(wt_all.shape, layer), _const_spec(bft.shape)]
    kv_shape = jax.ShapeDtypeStruct((n_fox, b, D_TOK, s), F32)
    return pl.pallas_call(
        functools.partial(_fox_in_prompt_kernel, tm=tm, aliased=aliased),
        grid=(b, s // tm),
        in_specs=in_specs,
        out_specs=[tr(D_TOK), stacked, stacked, tok(D_TOK), tr(HEADS_PAD), tok(LANES), tok(MEM_WIDTH)],
        out_shape=[jax.ShapeDtypeStruct((b, D_TOK, s), BF16),
                   kv_shape, kv_shape,
                   jax.ShapeDtypeStruct((b, s, D_TOK), BF16),
                   jax.ShapeDtypeStruct((b, HEADS_PAD, s), F32),
                   jax.ShapeDtypeStruct((b, s, LANES), BF16),
                   jax.ShapeDtypeStruct((b, s, MEM_WIDTH), BF16)],
        scratch_shapes=[pltpu.VMEM((HEADS_PAD, LANES), F32)],
        input_output_aliases={0: 1, 1: 2} if aliased else {},
        compiler_params=_params(2),
        name="fox_in_prompt",
    )(*ins)


def _fox_in_sample_kernel(x_ref, g_ref, wt_ref, bft_ref,
                          q_ref, kn_ref, vn_ref, kt_ref, vt_ref, lft_ref, cq_ref, *, t_new, db):
    h = _rmsnorm(x_ref[...], g_ref[...]).astype(BF16)
    allt = _dot_nt(wt_ref[...], h)
    q_ref[...] = jnp.transpose(allt[:D_TOK])
    kn_ref[...] = jnp.transpose(allt[D_TOK:2 * D_TOK])
    vn_ref[...] = jnp.transpose(allt[2 * D_TOK:3 * D_TOK])
    cq_ref[...] = jnp.transpose(allt[3 * D_TOK:3 * D_TOK + MEM_WIDTH])
    for t in range(t_new):
        kt_ref[t] = allt[D_TOK:2 * D_TOK, t * db:(t + 1) * db]
        vt_ref[t] = allt[2 * D_TOK:3 * D_TOK, t * db:(t + 1) * db]
    lft_ref[...] = _log_sigmoid(allt[3 * D_TOK + MEM_WIDTH:] + bft_ref[...])


def _fox_in_sample(x, g, wt_all, bft, layer, t_new, db):
    n, d = x.shape
    outs = [jax.ShapeDtypeStruct((n, D_TOK), F32),
            jax.ShapeDtypeStruct((n, D_TOK), F32),
            jax.ShapeDtypeStruct((n, D_TOK), F32),
            jax.ShapeDtypeStruct((t_new, D_TOK, db), F32),
            jax.ShapeDtypeStruct((t_new, D_TOK, db), F32),
            jax.ShapeDtypeStruct((HEADS_PAD, n), F32),
            jax.ShapeDtypeStruct((n, MEM_WIDTH), F32)]
    full = lambda a: pl.BlockSpec(a.shape, lambda i: (0,) * len(a.shape))
    return pl.pallas_call(
        functools.partial(_fox_in_sample_kernel, t_new=t_new, db=db),
        grid=(1,),
        in_specs=[full(x), full(g), _layer_spec(wt_all.shape, layer), full(bft)],
        out_specs=[full(o) for o in outs],
        out_shape=outs,
        compiler_params=_params(1),
        name="fox_in_sample",
    )(x, g, wt_all, bft)


UNROLL = 4
V_ROWS = HEAD_DIM + 2 * SUBLANES


def _fox_flash_kernel(qt_ref, kn_ref, cb_ref, vt_ref, o_ref, va_ref, sa_ref, sb_ref, *, tq, ck):
    hp = pl.program_id(1)
    qi = pl.program_id(2)
    s_len = va_ref.shape[-1]

    @pl.when(qi == 0)
    def _():
        ones_row = lax.broadcasted_iota(jnp.int32, (V_ROWS - HEAD_DIM, s_len), 0) == 0
        for hh in range(2):
            va_ref[hh, :HEAD_DIM, :] = vt_ref[0, 0, hh * HEAD_DIM:(hh + 1) * HEAD_DIM, :].astype(BF16)
            va_ref[hh, HEAD_DIM:, :] = jnp.where(ones_row, 1.0, 0.0).astype(BF16)

    qt = qt_ref[0]
    sub = lax.broadcasted_iota(jnp.int32, (2 * HEAD_DIM, tq), 0)
    row = lax.broadcasted_iota(jnp.int32, (ck, ck), 0)
    col = lax.broadcasted_iota(jnp.int32, (ck, ck), 1)
    n_diag = tq // ck
    qas = []
    for hh in range(2):
        head = 2 * hp + hh
        in_head = (sub >= HEAD_DIM) if hh else (sub < HEAD_DIM)
        q_top = jnp.where(in_head, qt, jnp.zeros_like(qt))
        pick = (sub == head) | (sub == head + HEADS_PAD) | (sub == head + 2 * HEADS_PAD)
        q_bot = jnp.where(pick, -1.0, 0.0).astype(BF16)
        qas.append(jnp.concatenate([q_top, q_bot], axis=0))
    qa = jnp.concatenate(qas, axis=1)

    def keys(j):
        off = pl.multiple_of(j * ck, ck)
        return jnp.concatenate([kn_ref[0, pl.ds(off, ck), :], cb_ref[0, pl.ds(off, ck), :]], axis=1)

    def logits(j, st_ref, lo=0):
        ka = keys(j)
        if lo == 0:
            st_ref[...] = _dot(ka, qa)
        else:
            for hh in range(2):
                st_ref[:, hh * tq + lo:(hh + 1) * tq] = _dot(ka, qa[:, hh * tq + lo:(hh + 1) * tq])

    def update(j, st_ref, carry, diag):
        first_blk = 0 if diag is None else diag
        off = pl.multiple_of(j * ck, ck)
        out = list(carry)
        for hh in range(2):
            vh = va_ref[hh, :, pl.ds(off, ck)]
            for blk in range(first_blk, n_diag):
                idx = 2 * (hh * n_diag + blk)
                m, acc = carry[idx], carry[idx + 1]
                st = st_ref[:, hh * tq + blk * ck:hh * tq + (blk + 1) * ck]
                if diag is not None and blk == diag:
                    st = jnp.where(row <= col, st, NEG)
                m_new = jnp.maximum(m, jnp.max(st, axis=0, keepdims=True))
                alpha = jnp.exp2(m - m_new)
                p = jnp.exp2(st - m_new).astype(BF16)
                out[idx] = m_new
                out[idx + 1] = alpha * acc + _dot(vh, p)
        return tuple(out)

    def body(i, carry):
        for u in range(0, UNROLL, 2):
            logits(UNROLL * i + u + 1, sb_ref)
            carry = update(UNROLL * i + u, sa_ref, carry, None)
            logits(UNROLL * i + u + 2, sa_ref)
            carry = update(UNROLL * i + u + 1, sb_ref, carry, None)
        return carry

    first = qi * n_diag
    m0 = jnp.full((1, ck), NEG, F32)
    a0 = jnp.zeros((V_ROWS, ck), F32)
    logits(0, sa_ref)
    carry = lax.fori_loop(0, first // UNROLL, body, (m0, a0) * (2 * n_diag))
    bufs = (sa_ref, sb_ref)
    for dg in range(n_diag):
        if dg + 1 < n_diag:
            logits(first + dg + 1, bufs[(dg + 1) % 2], (dg + 1) * ck)
        carry = update(first + dg, bufs[dg % 2], carry, dg)
    accs = [jnp.concatenate([carry[2 * (hh * n_diag + blk) + 1] for blk in range(n_diag)], axis=1)
            for hh in range(2)]
    outs = [acc[:HEAD_DIM] * (1.0 / acc[HEAD_DIM:HEAD_DIM + 1]) for acc in accs]
    o_ref[0] = jnp.transpose(jnp.concatenate(outs, axis=0)).astype(BF16)


def _fox_flash(qt, kn, cb, vt_all, layer, tq, ck):
    b, s, _ = kn.shape
    hw = 2 * HEAD_DIM
    return pl.pallas_call(
        functools.partial(_fox_flash_kernel, tq=tq, ck=ck),
        grid=(b, D_TOK // hw, s // tq),
        in_specs=[pl.BlockSpec((1, hw, tq), lambda i, h, j: (i, h, j)),
                  pl.BlockSpec((1, s, hw), lambda i, h, j: (i, 0, h)),
                  pl.BlockSpec((1, s, LANES), lambda i, h, j: (i, 0, 0)),
                  pl.BlockSpec((1, 1, hw, s), lambda i, h, j: (layer, i, h, 0))],
        out_specs=pl.BlockSpec((1, tq, hw), lambda i, h, j: (i, j, h)),
        out_shape=jax.ShapeDtypeStruct((b, s, D_TOK), BF16),
        scratch_shapes=[pltpu.VMEM((2, V_ROWS, s), BF16),
                        pltpu.VMEM((ck, 2 * tq), F32), pltpu.VMEM((ck, 2 * tq), F32)],
        compiler_params=_params(3),
        name="fox_flash",
    )(qt, kn, cb, vt_all)


def _fox_decode_kernel(pt_ref, q_ref, kn_ref, vn_ref, lfn_ref, *refs, n_pages, t_new):
    del pt_ref
    kp = refs[:n_pages]
    vp = refs[n_pages:2 * n_pages]
    lp = refs[2 * n_pages:3 * n_pages]
    o_ref = refs[3 * n_pages]
    rows = t_new * HEADS_PAD

    q8 = q_ref[0]
    hrow = lax.broadcasted_iota(jnp.int32, (HEADS_PAD, D_TOK), 0)
    hcol = lax.broadcasted_iota(jnp.int32, (HEADS_PAD, D_TOK), 1) // HEAD_DIM
    head_mask = jnp.where(hrow == hcol, 1.0, 0.0)
    qs = jnp.concatenate([jnp.broadcast_to(q8[t:t + 1, :], (HEADS_PAD, D_TOK)) * head_mask
                          for t in range(t_new)], axis=0)
    qsb = qs.astype(BF16)

    x = jnp.concatenate([lp[p][0, 0] for p in range(n_pages)], axis=0)
    r = _dot_f32_lhs(x, _tri_ones())
    offs = jnp.zeros((HEADS_PAD, LANES), F32)
    s_pages = []
    for p in range(n_pages):
        c_page = r[p * HEADS_PAD:(p + 1) * HEADS_PAD, :LANES] + offs
        offs = offs + r[p * HEADS_PAD:(p + 1) * HEADS_PAD, LANES:]
        kpg = kp[p][0, 0].reshape(D_TOK, LANES).astype(BF16)
        s = _dot(qsb, kpg) * SCALE
        s_pages.append(s - jnp.concatenate([c_page] * t_new, axis=0))

    kn = kn_ref[0]
    vn = vn_ref[0]
    lfn = lfn_ref[0]
    trow = lax.broadcasted_iota(jnp.int32, (rows, 1), 0) // HEADS_PAD
    c_new = offs[:, :1]
    s_new = []
    for t in range(t_new):
        c_new = c_new + lfn[:, t:t + 1]
        st = jnp.sum(qs * kn[t:t + 1, :], axis=-1, keepdims=True) * SCALE
        st = st - jnp.concatenate([c_new] * t_new, axis=0)
        s_new.append(jnp.where(trow >= t, st, NEG))

    mx = s_pages[0]
    for p in range(1, n_pages):
        mx = jnp.maximum(mx, s_pages[p])
    m = jnp.max(mx, axis=-1, keepdims=True)
    for t in range(t_new):
        m = jnp.maximum(m, s_new[t])

    lsum = jnp.zeros((rows, LANES), F32)
    o = jnp.zeros((rows, D_TOK), F32)
    for p in range(n_pages):
        pp = jnp.exp(s_pages[p] - m)
        lsum = lsum + pp
        vpg = vp[p][0, 0].reshape(D_TOK, LANES).astype(BF16)
        o = o + _dot_nt(pp.astype(BF16), vpg)
    l = jnp.sum(lsum, axis=-1, keepdims=True)
    for t in range(t_new):
        pn = jnp.exp(s_new[t] - m)
        l = l + pn
        o = o + pn.astype(BF16).astype(F32) * vn[t:t + 1, :]
    o = o * (1.0 / l)
    outs = [jnp.sum(o[t * HEADS_PAD:(t + 1) * HEADS_PAD] * head_mask, axis=0, keepdims=True)
            for t in range(t_new)]
    outs.append(jnp.zeros((SUBLANES - t_new, D_TOK), F32))
    o_ref[0] = jnp.concatenate(outs, axis=0)


def _fox_decode(page_table_flat, q8, kn8, vn8, lfn, ck, cv, clf, layer, n_pages, t_new):
    db = q8.shape[0]
    page = ck.shape[-1]
    tok = pl.BlockSpec((1, SUBLANES, D_TOK), lambda i, pt: (i, 0, 0))
    kv_spec = lambda p: pl.BlockSpec((1, 1, FOX_HEADS, HEAD_DIM, page),
                                     lambda i, pt: (layer, pt[i * n_pages + p], 0, 0, 0))
    lf_spec = lambda p: pl.BlockSpec((1, 1, HEADS_PAD, page),
                                     lambda i, pt: (layer, pt[i * n_pages + p], 0, 0))
    grid_spec = pltpu.PrefetchScalarGridSpec(
        num_scalar_prefetch=1,
        grid=(db,),
        in_specs=[tok, tok, tok, pl.BlockSpec((1, HEADS_PAD, SUBLANES), lambda i, pt: (i, 0, 0))]
        + [kv_spec(p) for p in range(n_pages)] + [kv_spec(p) for p in range(n_pages)]
        + [lf_spec(p) for p in range(n_pages)],
        out_specs=tok,
    )
    return pl.pallas_call(
        functools.partial(_fox_decode_kernel, n_pages=n_pages, t_new=t_new),
        grid_spec=grid_spec,
        out_shape=jax.ShapeDtypeStruct((db, SUBLANES, D_TOK), F32),
        compiler_params=_params(1),
        name="fox_decode",
    )(page_table_flat, q8, kn8, vn8, lfn, *([ck] * n_pages), *([cv] * n_pages), *([clf] * n_pages))


def _lru_gates(xc, gate_refs, lam_ref):
    wr_ref, br_ref, wi_ref, bi_ref = gate_refs
    xcb = xc.astype(BF16)
    nblk = D_TOK // GATE_BLOCK
    sl = lambda i: xcb[:, i * GATE_BLOCK:(i + 1) * GATE_BLOCK]
    r_pre = jnp.concatenate([_dot(sl(i), wr_ref[i]) for i in range(nblk)], axis=1) + br_ref[...]
    i_pre = jnp.concatenate([_dot(sl(i), wi_ref[i]) for i in range(nblk)], axis=1) + bi_ref[...]
    r = jax.nn.sigmoid(r_pre)
    ig = jax.nn.sigmoid(i_pre)
    log_a = -LRU_C * r * _softplus(-lam_ref[...])
    a = jnp.exp(log_a)
    u = jnp.sqrt(-jnp.tanh(log_a) * (a * a + 1.0)) * ig * xc
    return a, u


def _lru_prompt_kernel(x_ref, g_ref, w_ref, cw_ref, cb_ref, wr_ref, br_ref, wi_ref, bi_ref, lam_ref,
                       y_ref, cq_ref, hl_ref, cs_ref, tail_ref, hc_ref, *, tm):
    @pl.when(pl.program_id(1) == 0)
    def _():
        tail_ref[...] = jnp.zeros_like(tail_ref)
        hc_ref[...] = jnp.zeros_like(hc_ref)

    h = _rmsnorm(x_ref[0], g_ref[...]).astype(BF16)
    z = _dot(h, w_ref[...])
    gate = z[:, :D_TOK]
    xb = z[:, D_TOK:2 * D_TOK]
    cq_ref[0] = z[:, 2 * D_TOK:].astype(BF16)
    ext = jnp.concatenate([tail_ref[...], xb], axis=0)
    xc = cb_ref[...] + xb * cw_ref[CONV_WIDTH - 1:CONV_WIDTH, :]
    for i in range(CONV_WIDTH - 1):
        back = CONV_WIDTH - 1 - i
        xc = xc + pltpu.roll(ext, back, axis=0)[SUBLANES:, :] * cw_ref[i:i + 1, :]
    tail_ref[...] = xb[tm - SUBLANES:, :]
    cs_ref[0] = xb[tm - (CONV_WIDTH - 1):, :]

    a, u = _lru_gates(xc, (wr_ref, br_ref, wi_ref, bi_ref), lam_ref)
    groups = tm // SUBLANES
    a3 = a.reshape(groups, SUBLANES, D_TOK)
    u3 = u.reshape(groups, SUBLANES, D_TOK)
    sub = lax.broadcasted_iota(jnp.int32, (groups, SUBLANES, D_TOK), 1)
    sh = 1
    while sh < SUBLANES:
        keep = sub >= sh
        a_sh = jnp.where(keep, pltpu.roll(a3, sh, axis=1), 1.0)
        u_sh = jnp.where(keep, pltpu.roll(u3, sh, axis=1), 0.0)
        u3 = u3 + a3 * u_sh
        a3 = a3 * a_sh
        sh *= 2
    hc = hc_ref[...]
    hs = []
    for gi in range(groups):
        hg = u3[gi] + a3[gi] * hc
        hs.append(hg)
        hc = hg[SUBLANES - 1:SUBLANES, :]
    hs = jnp.concatenate(hs, axis=0)
    hc_ref[...] = hc
    hl_ref[0] = hc
    y_ref[0] = (_gelu_tanh(gate) * hs).astype(BF16)


def _lru_prompt(x, g, w_all, cw, cb, wr, br, wi, bi, lam, layer, tm):
    b, s, d = x.shape
    tok = lambda wd: pl.BlockSpec((1, tm, wd), lambda i, j: (i, j, 0))
    consts = (g, w_all, cw, cb, wr, br, wi, bi, lam)
    specs = [_const_spec(c.shape) for c in consts]
    specs[1] = _layer_spec(w_all.shape, layer)
    return pl.pallas_call(
        functools.partial(_lru_prompt_kernel, tm=tm),
        grid=(b, s // tm),
        in_specs=[tok(d)] + specs,
        out_specs=[tok(D_TOK), tok(MEM_WIDTH),
                   pl.BlockSpec((1, 1, D_TOK), lambda i, j: (i, 0, 0)),
                   pl.BlockSpec((1, CONV_WIDTH - 1, D_TOK), lambda i, j: (i, 0, 0))],
        out_shape=[jax.ShapeDtypeStruct((b, s, D_TOK), BF16),
                   jax.ShapeDtypeStruct((b, s, MEM_WIDTH), BF16),
                   jax.ShapeDtypeStruct((b, 1, D_TOK), F32),
                   jax.ShapeDtypeStruct((b, CONV_WIDTH - 1, D_TOK), F32)],
        scratch_shapes=[pltpu.VMEM((SUBLANES, D_TOK), F32), pltpu.VMEM((1, D_TOK), F32)],
        compiler_params=_params(2),
        name="lru_prompt",
    )(x, *consts)


def _lru_sample_kernel(x_ref, g_ref, w_ref, cw_ref, cb_ref, wr_ref, br_ref, wi_ref, bi_ref, lam_ref,
                       h0_ref, buf_ref, y_ref, cq_ref, hl_ref, nb_ref, *, t_new, db):
    h = _rmsnorm(x_ref[...], g_ref[...]).astype(BF16)
    z = _dot(h, w_ref[...])
    gate = z[:, :D_TOK]
    xb = z[:, D_TOK:2 * D_TOK]
    cq_ref[...] = z[:, 2 * D_TOK:]
    xp = [buf_ref[i] for i in range(CONV_WIDTH - 1)] + [xb[t * db:(t + 1) * db] for t in range(t_new)]
    xc_t = []
    for t in range(t_new):
        acc = cb_ref[...] + xp[t] * cw_ref[0:1, :]
        for i in range(1, CONV_WIDTH):
            acc = acc + xp[t + i] * cw_ref[i:i + 1, :]
        xc_t.append(acc)
    for i in range(CONV_WIDTH - 1):
        nb_ref[i] = xp[t_new + i]
    xc = jnp.concatenate(xc_t, axis=0)
    a, u = _lru_gates(xc, (wr_ref, br_ref, wi_ref, bi_ref), lam_ref)
    hc = h0_ref[...]
    hs = []
    for t in range(t_new):
        hc = a[t * db:(t + 1) * db] * hc + u[t * db:(t + 1) * db]
        hs.append(hc)
    hl_ref[...] = hc
    y_ref[...] = _gelu_tanh(gate) * jnp.concatenate(hs, axis=0)


def _lru_sample(x, g, w_all, cw, cb, wr, br, wi, bi, lam, h0, buf, layer, t_new, db):
    n, d = x.shape
    ins = (x, g, w_all, cw, cb, wr, br, wi, bi, lam, h0, buf)
    outs = [jax.ShapeDtypeStruct((n, D_TOK), F32),
            jax.ShapeDtypeStruct((n, MEM_WIDTH), F32),
            jax.ShapeDtypeStruct((db, D_TOK), F32),
            jax.ShapeDtypeStruct((CONV_WIDTH - 1, db, D_TOK), F32)]
    full = lambda a: pl.BlockSpec(a.shape, lambda i: (0,) * len(a.shape))
    return pl.pallas_call(
        functools.partial(_lru_sample_kernel, t_new=t_new, db=db),
        grid=(1,),
        in_specs=[_layer_spec(a.shape, layer) if a is w_all else full(a) for a in ins],
        out_specs=[full(o) for o in outs],
        out_shape=outs,
        compiler_params=_params(1),
        name="lru_sample",
    )(*ins)


def _cross_sample_kernel(q_ref, mk_ref, mv_ref, o_ref, *, t_new, group):
    hpad = SUBLANES
    hrow = lax.broadcasted_iota(jnp.int32, (hpad, MEM_WIDTH), 0)
    hcol = lax.broadcasted_iota(jnp.int32, (hpad, MEM_WIDTH), 1) // HEAD_DIM
    head_mask = jnp.where(hrow == hcol, 1.0, 0.0)
    for i in range(group):
        q8 = q_ref[i]
        qs = jnp.concatenate([jnp.broadcast_to(q8[t:t + 1, :], (hpad, MEM_WIDTH)) * head_mask
                              for t in range(t_new)], axis=0)
        s = _dot(qs.astype(BF16), mk_ref[0, i].astype(BF16)) * SCALE
        e = jnp.exp(s - jnp.max(s, axis=-1, keepdims=True))
        p = e * (1.0 / jnp.sum(e, axis=-1, keepdims=True))
        o = _dot_nt(p.astype(BF16), mv_ref[0, i].astype(BF16))
        outs = [jnp.sum(o[t * hpad:(t + 1) * hpad] * head_mask, axis=0, keepdims=True)
                for t in range(t_new)]
        outs.append(jnp.zeros((SUBLANES - t_new, MEM_WIDTH), F32))
        o_ref[i] = jnp.concatenate(outs, axis=0)


def _cross_sample(q8, cmk, cmv, layer, t_new):
    db = q8.shape[0]
    n_mem = cmk.shape[-1]
    group = SUBLANES
    tok = pl.BlockSpec((group, SUBLANES, MEM_WIDTH), lambda i: (i, 0, 0))
    kv = pl.BlockSpec((1, group, MEM_WIDTH, n_mem), lambda i: (layer, i, 0, 0))
    return pl.pallas_call(
        functools.partial(_cross_sample_kernel, t_new=t_new, group=group),
        grid=(db // group,),
        in_specs=[tok, kv, kv],
        out_specs=tok,
        out_shape=jax.ShapeDtypeStruct((db, SUBLANES, MEM_WIDTH), F32),
        compiler_params=_params(1),
        name="cross_sample",
    )(q8, cmk, cmv)


def _cross_attend(cq, mk, mv):
    tm = cq.shape[0]
    mk = mk.astype(BF16)
    mv = mv.astype(BF16)
    lane_head = lax.broadcasted_iota(jnp.int32, (tm, MEM_WIDTH), 1) // HEAD_DIM
    cross = jnp.zeros((tm, MEM_WIDTH), F32)
    for hd in range(MEM_HEADS):
        qh = jnp.where(lane_head == hd, cq, jnp.zeros_like(cq))
        s = _dot(qh, mk) * SCALE
        e = jnp.exp(s - jnp.max(s, axis=-1, keepdims=True))
        p = e * (1.0 / jnp.sum(e, axis=-1, keepdims=True))
        cross = jnp.where(lane_head == hd, _dot_nt(p.astype(BF16), mv), cross)
    return cross.astype(BF16)


def _finish_kernel(*refs, attend, final):
    if attend:
        x_ref, yt_ref, c_ref, mk_ref, mv_ref, wo_ref, g_ref, wu_ref, wd_ref, gf_ref, o_ref = refs
        cross = _cross_attend(c_ref[0], mk_ref[...], mv_ref[...])
    else:
        x_ref, yt_ref, c_ref, wo_ref, g_ref, wu_ref, wd_ref, gf_ref, o_ref = refs
        cross = c_ref[0]
    y = _dot(yt_ref[0], wo_ref[:D_TOK, :]) + _dot(cross, wo_ref[D_TOK:, :])
    x1 = x_ref[0] + y
    hm = _rmsnorm(x1, g_ref[...]).astype(BF16)
    acc = x1
    d_ff = wu_ref.shape[1]
    for c in range(d_ff // FF_CHUNK):
        up = _dot(hm, wu_ref[:, c * FF_CHUNK:(c + 1) * FF_CHUNK])
        act = jnp.square(jnp.maximum(up, 0.0)).astype(BF16)
        acc = acc + _dot(act, wd_ref[c * FF_CHUNK:(c + 1) * FF_CHUNK, :])
    if final:
        acc = _rmsnorm(acc, gf_ref[...])
    o_ref[0] = acc


def _finish(x, yt, c, mk_all, mv_all, wo_all, g_all, wu_all, wd_all, gf, layer, tm, final):
    b, s, d = x.shape
    attend = mk_all is not None
    tok = lambda wdt: pl.BlockSpec((1, tm, wdt), lambda i, j: (i, j, 0))
    ins = [x, yt, c]
    in_specs = [tok(d), tok(D_TOK), tok(MEM_WIDTH)]
    if attend:
        ins += [mk_all, mv_all]
        in_specs += [pl.BlockSpec((None, None) + mk_all.shape[2:], lambda i, j: (layer, i, 0, 0))] * 2
    ins += [wo_all, g_all, wu_all, wd_all, gf]
    in_specs += [_layer_spec(wo_all.shape, layer), _layer_spec(g_all.shape, layer),
                 _layer_spec(wu_all.shape, layer), _layer_spec(wd_all.shape, layer), _const_spec(gf.shape)]
    return pl.pallas_call(
        functools.partial(_finish_kernel, attend=attend, final=final),
        grid=(b, s // tm),
        in_specs=in_specs,
        out_specs=tok(d),
        out_shape=jax.ShapeDtypeStruct((b, s, d), F32),
        compiler_params=_params(2),
        name="finish_prompt" if attend else "finish_sample",
    )(*ins)


def _pad_rows(a, rows):
    return jnp.pad(a, ((0, 0), (0, rows - a.shape[1]), (0, 0)))


def _block_diag(w):
    per = GATE_BLOCK // HEAD_DIM
    w = w.reshape(D_TOK // GATE_BLOCK, per, HEAD_DIM, HEAD_DIM)
    eye = jnp.eye(per, dtype=w.dtype)
    out = jnp.einsum('gpij,pq->gpiqj', w, eye)
    return out.reshape(D_TOK // GATE_BLOCK, GATE_BLOCK, GATE_BLOCK).astype(BF16)


def kernel(x_prompt, x_sample, mem_prompt, cache_fox_k, cache_fox_v, cache_fox_logf, state_lru_h, state_lru_conv, cache_mem_k, cache_mem_v, page_table, norm_mix, norm_mem, norm_mlp, norm_final, w_in_fox, b_forget, w_in_lru, conv_w, conv_b, w_rgate, b_rgate, w_igate, b_igate, lru_lambda, w_mem_kv, w_out, w_up, w_down):
    b, s, d = x_prompt.shape
    db, t_new, _ = x_sample.shape
    depth = norm_mix.shape[0]
    n_pages = page_table.shape[1]
    n_mem = mem_prompt.shape[1]
    tm = min(ROW_TILE, s)
    n_s = t_new * db
    assert s % tm == 0 and tm % LANES == 0 and t_new <= SUBLANES and db % SUBLANES == 0
    assert s % FLASH_TQ == 0 and FLASH_TQ % (UNROLL * FLASH_CK) == 0 and UNROLL % 2 == 0

    row = lambda v: v.reshape(1, -1)
    xp = x_prompt
    xs = jnp.transpose(x_sample, (1, 0, 2)).reshape(n_s, d)
    ck = jnp.transpose(cache_fox_k, (0, 1, 3, 4, 2))
    cv = jnp.transpose(cache_fox_v, (0, 1, 3, 4, 2))
    clf = jnp.pad(jnp.transpose(cache_fox_logf, (0, 1, 3, 2)),
                  ((0, 0), (0, 0), (0, HEADS_PAD - FOX_HEADS), (0, 0)))
    cmk = jnp.transpose(cache_mem_k, (0, 1, 3, 4, 2)).reshape(depth, db, MEM_WIDTH, n_mem)
    cmv = jnp.transpose(cache_mem_v, (0, 1, 3, 4, 2)).reshape(depth, db, MEM_WIDTH, n_mem)
    pt_flat = page_table.reshape(-1)

    mk_all, mv_all = _mem_kv(mem_prompt, norm_mem, jnp.transpose(w_mem_kv, (0, 2, 1)).astype(BF16))

    wo_all = w_out.astype(BF16)
    wu_all = w_up.astype(BF16)
    wd_all = w_down.astype(BF16)
    w_lru_all = w_in_lru.astype(BF16)
    g_mlp_all = norm_mlp.reshape(depth, 1, d)
    g_fin = row(norm_final)
    wall_t = jnp.transpose(w_in_fox, (2, 0, 1))
    n_qkv = 3 * D_TOK
    w_fox_all = jnp.transpose(jnp.concatenate(
        [wall_t[:n_qkv], wall_t[n_qkv + FOX_HEADS:],
         jnp.pad(wall_t[n_qkv:n_qkv + FOX_HEADS], ((0, HEADS_PAD - FOX_HEADS), (0, 0), (0, 0)))],
        axis=0), (1, 0, 2)).astype(BF16)

    def to_seq(a):
        return _pad_rows(jnp.transpose(a.reshape(t_new, db, -1), (1, 0, 2)), SUBLANES)

    def from_seq(a):
        return jnp.transpose(a[:, :t_new], (1, 0, 2)).reshape(n_s, -1).astype(BF16)

    rnd = lambda a: a.astype(BF16).astype(F32)
    pf, sk, sv, sf = [], [], [], []
    n_fox = (depth + 1) // 2
    kt_all = vt_all = None
    ph, pc, sh, sc = [], [], [], []
    for l in range(depth):
        j = l // 2
        final = l == depth - 1
        g_mix = row(norm_mix[l])
        fin_w = (wo_all, g_mlp_all, wu_all, wd_all, g_fin, l)
        mem_l = (mk_all, mv_all)
        if l % 2 == 0:
            bft = jnp.pad(b_forget[j], (0, HEADS_PAD - FOX_HEADS)).reshape(HEADS_PAD, 1)
            qt_p, kt_all, vt_all, kn_p, lft_p, cb_p, cq_p = _fox_in_prompt(
                xp, g_mix, w_fox_all, bft, tm, j, n_fox, kt_all, vt_all)
            yt_p = _fox_flash(qt_p, kn_p, cb_p, vt_all, j, FLASH_TQ, FLASH_CK)
            pf.append(jnp.transpose(lft_p[:, :FOX_HEADS], (0, 2, 1)))

            q_s, kn_s, vn_s, kt_s, vt_s, lft_s, cq_s = _fox_in_sample(xs, g_mix, w_fox_all, bft, j, t_new, db)
            lfn = jnp.pad(jnp.transpose(lft_s.reshape(HEADS_PAD, t_new, db), (2, 0, 1)),
                          ((0, 0), (0, 0), (0, SUBLANES - t_new)))
            xp = _finish(xp, yt_p, cq_p, *mem_l, *fin_w, tm, final)
            yt_s8 = _fox_decode(pt_flat, to_seq(rnd(q_s)), to_seq(rnd(kn_s)), to_seq(rnd(vn_s)), lfn,
                                ck, cv, clf, j, n_pages, t_new)
            yt_s = from_seq(yt_s8)
            sk.append(jnp.transpose(kt_s.reshape(t_new, FOX_HEADS, HEAD_DIM, db), (3, 0, 1, 2)))
            sv.append(jnp.transpose(vt_s.reshape(t_new, FOX_HEADS, HEAD_DIM, db), (3, 0, 1, 2)))
            sf.append(jnp.transpose(lft_s[:FOX_HEADS].reshape(FOX_HEADS, t_new, db), (2, 1, 0)))
        else:
            consts = (g_mix, w_lru_all, conv_w[j], row(conv_b[j]),
                      _block_diag(w_rgate[j]), row(b_rgate[j]),
                      _block_diag(w_igate[j]), row(b_igate[j]), row(lru_lambda[j]))
            yt_p, cq_p, hl_p, cs_p = _lru_prompt(xp, *consts, j, tm)
            ph.append(hl_p.reshape(b, D_TOK))
            pc.append(cs_p)
            yt_s, cq_s, hl_s, nb_s = _lru_sample(xs, *consts, state_lru_h[j],
                                                 jnp.transpose(state_lru_conv[j], (1, 0, 2)), j, t_new, db)
            yt_s = yt_s.astype(BF16)
            sh.append(hl_s)
            sc.append(jnp.transpose(nb_s, (1, 0, 2)))
            xp = _finish(xp, yt_p, cq_p, *mem_l, *fin_w, tm, final)

        cross_s = from_seq(_cross_sample(to_seq(rnd(cq_s)), cmk, cmv, l, t_new))
        xs = _finish(xs.reshape(1, n_s, d), yt_s.reshape(1, n_s, D_TOK), cross_s.reshape(1, n_s, MEM_WIDTH),
                     None, None, *fin_w, n_s, final).reshape(n_s, d)

    y_sample = jnp.transpose(xs.reshape(t_new, db, d), (1, 0, 2))
    mem_out = lambda m: jnp.transpose(m.reshape(depth, b, MEM_HEADS, HEAD_DIM, n_mem), (0, 1, 4, 2, 3))
    fox_out = lambda a: jnp.transpose(a.reshape(n_fox, b, FOX_HEADS, HEAD_DIM, s), (0, 1, 4, 2, 3))
    return (xp, y_sample,
            fox_out(kt_all), fox_out(vt_all), jnp.stack(pf),
            jnp.stack(sk), jnp.stack(sv), jnp.stack(sf),
            jnp.stack(ph), jnp.stack(pc), jnp.stack(sh), jnp.stack(sc),
            mem_out(mk_all), mem_out(mv_all))
```

```python
import functools

import jax
import jax.numpy as jnp
from jax import lax
from jax.experimental import pallas as pl
from jax.experimental.pallas import tpu as pltpu

F32 = jnp.float32
BF16 = jnp.bfloat16

HEAD_DIM = 64
D_TOK = 768
MEM_WIDTH = 256
FOX_HEADS = D_TOK // HEAD_DIM
MEM_HEADS = MEM_WIDTH // HEAD_DIM
HEADS_PAD = 16
FOX_ROWS = 3 * D_TOK + MEM_WIDTH + HEADS_PAD
CONV_WIDTH = 4
LRU_C = 8.0
GATE_BLOCK = 256
EPS = 1e-6
SCALE = HEAD_DIM ** -0.5
LOG2E = 1.4426950408889634
NEG = -1e30
LANES = 128
SUBLANES = 8
ROW_TILE = 512
FF_CHUNK = 1024
FLASH_TQ = 2048
FLASH_CK = 256
VMEM_LIMIT = 56 * 1024 * 1024


def _dot(a, b):
    return jnp.dot(a, b, preferred_element_type=F32)


def _dot_nt(a, b):
    return lax.dot_general(a, b, (((1,), (1,)), ((), ())), preferred_element_type=F32)


def _rmsnorm(x, g):
    return x * lax.rsqrt(jnp.mean(x * x, axis=-1, keepdims=True) + EPS) * g


def _softplus(x):
    return jnp.maximum(x, 0.0) + jnp.log1p(jnp.exp(-jnp.abs(x)))


def _log_sigmoid(x):
    return -_softplus(-x)


def _gelu_tanh(x):
    return x * (0.5 * (1.0 + jnp.tanh(0.7978845608028654 * (x + 0.044715 * (x * x * x)))))


def _tri_ones():
    r = lax.broadcasted_iota(jnp.int32, (LANES, 2 * LANES), 0)
    c = lax.broadcasted_iota(jnp.int32, (LANES, 2 * LANES), 1)
    return jnp.where((c >= LANES) | (r <= c), 1.0, 0.0).astype(BF16)


def _split_bf16(x):
    hi = x.astype(BF16)
    r1 = x - hi.astype(F32)
    mid = r1.astype(BF16)
    lo = (r1 - mid.astype(F32)).astype(BF16)
    return hi, mid, lo


def _dot_f32_lhs(x, m):
    hi, mid, lo = _split_bf16(x)
    return _dot(hi, m) + _dot(mid, m) + _dot(lo, m)


def _const_spec(shape):
    nd = len(shape)
    return pl.BlockSpec(shape, lambda *_: (0,) * nd, pipeline_mode=pl.Buffered(1))


def _layer_spec(shape, layer):
    nd = len(shape) - 1
    return pl.BlockSpec((None,) + tuple(shape[1:]), lambda *_: (layer,) + (0,) * nd,
                        pipeline_mode=pl.Buffered(1))


def _params(n_axes):
    return pltpu.CompilerParams(dimension_semantics=("arbitrary",) * n_axes,
                                vmem_limit_bytes=VMEM_LIMIT)


def _memkv_kernel(m_ref, g_ref, w_ref, k_ref, v_ref):
    hm = _rmsnorm(m_ref[0], g_ref[0]).astype(BF16)
    kvt = _dot_nt(w_ref[0], hm)
    k_ref[0, 0] = kvt[:MEM_WIDTH]
    v_ref[0, 0] = kvt[MEM_WIDTH:]


def _mem_kv(mem, norm_mem, w_kv_t):
    depth = norm_mem.shape[0]
    b, n_mem, d = mem.shape
    out = jax.ShapeDtypeStruct((depth, b, MEM_WIDTH, n_mem), F32)
    return pl.pallas_call(
        _memkv_kernel,
        grid=(depth, b),
        in_specs=[pl.BlockSpec((1, n_mem, d), lambda l, i: (i, 0, 0)),
                  pl.BlockSpec((1, 1, d), lambda l, i: (l, 0, 0)),
                  pl.BlockSpec((1, 2 * MEM_WIDTH, d), lambda l, i: (l, 0, 0))],
        out_specs=[pl.BlockSpec((1, 1, MEM_WIDTH, n_mem), lambda l, i: (l, i, 0, 0))] * 2,
        out_shape=[out, out],
        compiler_params=_params(2),
        name="mem_kv",
    )(mem, norm_mem.reshape(depth, 1, d), w_kv_t)


def _fox_in_prompt_kernel(*refs, tm, aliased):
    if aliased:
        refs = refs[2:]
    (x_ref, g_ref, wt_ref, bft_ref,
     qt_ref, kt_ref, vt_ref, kn_ref, lft_ref, cb_ref, cq_ref, carry_ref) = refs

    @pl.when(pl.program_id(1) == 0)
    def _():
        carry_ref[...] = jnp.zeros_like(carry_ref)

    h = _rmsnorm(x_ref[0], g_ref[...]).astype(BF16)
    allt = _dot_nt(wt_ref[...], h)
    qt_ref[0] = (allt[:D_TOK] * (SCALE * LOG2E)).astype(BF16)
    kt = allt[D_TOK:2 * D_TOK]
    kt_ref[0, 0] = kt
    vt_ref[0, 0] = allt[2 * D_TOK:3 * D_TOK]
    kn_ref[0] = jnp.transpose(kt).astype(BF16)
    cq_ref[0] = jnp.transpose(allt[3 * D_TOK:3 * D_TOK + MEM_WIDTH]).astype(BF16)
    hrow = lax.broadcasted_iota(jnp.int32, (HEADS_PAD, tm), 0)
    lf = jnp.where(hrow < FOX_HEADS,
                   _log_sigmoid(allt[3 * D_TOK + MEM_WIDTH:] + bft_ref[...]), 0.0)
    lft_ref[0] = lf
    uo = _tri_ones()
    carry = carry_ref[...]
    pieces = []
    for blk in range(tm // LANES):
        r = _dot_f32_lhs(lf[:, blk * LANES:(blk + 1) * LANES], uo)
        pieces.append(carry + r[:, :LANES])
        carry = carry + r[:, LANES:]
    carry_ref[...] = carry
    ct = jnp.concatenate(pieces, axis=1)
    ct = jnp.concatenate([ct, jnp.zeros((LANES - HEADS_PAD, tm), F32)], axis=0)
    hi, mid, lo = _split_bf16(jnp.transpose(ct) * LOG2E)
    packed = (hi.astype(F32) + pltpu.roll(mid.astype(F32), HEADS_PAD, axis=1)
              + pltpu.roll(lo.astype(F32), 2 * HEADS_PAD, axis=1))
    cb_ref[0] = packed.astype(BF16)


def _fox_in_prompt(x, g, wt_all, bft, tm, layer, n_fox, kt_prev, vt_prev):
    b, s, d = x.shape
    aliased = kt_prev is not None
    tok = lambda w: pl.BlockSpec((1, tm, w), lambda i, j: (i, j, 0))
    tr = lambda r: pl.BlockSpec((1, r, tm), lambda i, j: (i, 0, j))
    stacked = pl.BlockSpec((1, 1, D_TOK, tm), lambda i, j: (layer, i, 0, j))
    ins = ([kt_prev, vt_prev] if aliased else []) + [x, g, wt_all, bft]
    in_specs = ([pl.BlockSpec(memory_space=pl.ANY)] * 2 if aliased else []) \
        + [tok(d), _const_spec(g.shape), _layer_spec(wt_all.shape, layer), _const_spec(bft.shape)]
    kv_shape = jax.ShapeDtypeStruct((n_fox, b, D_TOK, s), F32)
    return pl.pallas_call(
        functools.partial(_fox_in_prompt_kernel, tm=tm, aliased=aliased),
        grid=(b, s // tm),
        in_specs=in_specs,
        out_specs=[tr(D_TOK), stacked, stacked, tok(D_TOK), tr(HEADS_PAD), tok(LANES), tok(MEM_WIDTH)],
        out_shape=[jax.ShapeDtypeStruct((b, D_TOK, s), BF16),
                   kv_shape, kv_shape,
                   jax.ShapeDtypeStruct((b, s, D_TOK), BF16),
                   jax.ShapeDtypeStruct((b, HEADS_PAD, s), F32),
                   jax.ShapeDtypeStruct((b, s, LANES), BF16),
                   jax.ShapeDtypeStruct((b, s, MEM_WIDTH), BF16)],
        scratch_shapes=[pltpu.VMEM((HEADS_PAD, LANES), F32)],
        input_output_aliases={0: 1, 1: 2} if aliased else {},
        compiler_params=_params(2),
        name="fox_in_prompt",
    )(*ins)


def _fox_in_sample_kernel(x_ref, g_ref, wt_ref, bft_ref,
                          q_ref, kn_ref, vn_ref, kt_ref, vt_ref, lft_ref, cq_ref, *, t_new, db):
    h = _rmsnorm(x_ref[...], g_ref[...]).astype(BF16)
    allt = _dot_nt(wt_ref[...], h)
    q_ref[...] = jnp.transpose(allt[:D_TOK])
    kn_ref[...] = jnp.transpose(allt[D_TOK:2 * D_TOK])
    vn_ref[...] = jnp.transpose(allt[2 * D_TOK:3 * D_TOK])
    cq_ref[...] = jnp.transpose(allt[3 * D_TOK:3 * D_TOK + MEM_WIDTH])
    for t in range(t_new):
        kt_ref[t] = allt[D_TOK:2 * D_TOK, t * db:(t + 1) * db]
        vt_ref[t] = allt[2 * D_TOK:3 * D_TOK, t * db:(t + 1) * db]
    lft_ref[...] = _log_sigmoid(allt[3 * D_TOK + MEM_WIDTH:] + bft_ref[...])


def _fox_in_sample(x, g, wt_all, bft, layer, t_new, db):
    n, d = x.shape
    outs = [jax.ShapeDtypeStruct((n, D_TOK), F32),
            jax.ShapeDtypeStruct((n, D_TOK), F32),
            jax.ShapeDtypeStruct((n, D_TOK), F32),
            jax.ShapeDtypeStruct((t_new, D_TOK, db), F32),
            jax.ShapeDtypeStruct((t_new, D_TOK, db), F32),
            jax.ShapeDtypeStruct((HEADS_PAD, n), F32),
            jax.ShapeDtypeStruct((n, MEM_WIDTH), F32)]
    full = lambda a: pl.BlockSpec(a.shape, lambda i: (0,) * len(a.shape))
    return pl.pallas_call(
        functools.partial(_fox_in_sample_kernel, t_new=t_new, db=db),
        grid=(1,),
        in_specs=[full(x), full(g), _layer_spec(wt_all.shape, layer), full(bft)],
        out_specs=[full(o) for o in outs],
        out_shape=outs,
        compiler_params=_params(1),
        name="fox_in_sample",
    )(x, g, wt_all, bft)


UNROLL = 4
V_ROWS = HEAD_DIM + 2 * SUBLANES


def _fox_flash_kernel(qt_ref, kn_ref, cb_ref, vt_ref, o_ref, va_ref, sa_ref, sb_ref, *, tq, ck):
    hp = pl.program_id(1)
    qi = pl.program_id(2)
    s_len = va_ref.shape[-1]

    @pl.when(qi == 0)
    def _():
        ones_row = lax.broadcasted_iota(jnp.int32, (V_ROWS - HEAD_DIM, s_len), 0) == 0
        for hh in range(2):
            va_ref[hh, :HEAD_DIM, :] = vt_ref[0, 0, hh * HEAD_DIM:(hh + 1) * HEAD_DIM, :].astype(BF16)
            va_ref[hh, HEAD_DIM:, :] = jnp.where(ones_row, 1.0, 0.0).astype(BF16)

    qt = qt_ref[0]
    sub = lax.broadcasted_iota(jnp.int32, (2 * HEAD_DIM, tq), 0)
    row = lax.broadcasted_iota(jnp.int32, (ck, ck), 0)
    col = lax.broadcasted_iota(jnp.int32, (ck, ck), 1)
    n_diag = tq // ck
    qas = []
    for hh in range(2):
        head = 2 * hp + hh
        in_head = (sub >= HEAD_DIM) if hh else (sub < HEAD_DIM)
        q_top = jnp.where(in_head, qt, jnp.zeros_like(qt))
        pick = (sub == head) | (sub == head + HEADS_PAD) | (sub == head + 2 * HEADS_PAD)
        q_bot = jnp.where(pick, -1.0, 0.0).astype(BF16)
        qas.append(jnp.concatenate([q_top, q_bot], axis=0))
    qa = jnp.concatenate(qas, axis=1)

    def keys(j):
        off = pl.multiple_of(j * ck, ck)
        return jnp.concatenate([kn_ref[0, pl.ds(off, ck), :], cb_ref[0, pl.ds(off, ck), :]], axis=1)

    def logits(j, st_ref, lo=0):
        ka = keys(j)
        if lo == 0:
            st_ref[...] = _dot(ka, qa)
        else:
            for hh in range(2):
                st_ref[:, hh * tq + lo:(hh + 1) * tq] = _dot(ka, qa[:, hh * tq + lo:(hh + 1) * tq])

    def update(j, st_ref, carry, diag):
        first_blk = 0 if diag is None else diag
        off = pl.multiple_of(j * ck, ck)
        out = list(carry)
        for hh in range(2):
            vh = va_ref[hh, :, pl.ds(off, ck)]
            for blk in range(first_blk, n_diag):
                idx = 2 * (hh * n_diag + blk)
                m, acc = carry[idx], carry[idx + 1]
                st = st_ref[:, hh * tq + blk * ck:hh * tq + (blk + 1) * ck]
                if diag is not None and blk == diag:
                    st = jnp.where(row <= col, st, NEG)
                m_new = jnp.maximum(m, jnp.max(st, axis=0, keepdims=True))
                alpha = jnp.exp2(m - m_new)
                p = jnp.exp2(st - m_new).astype(BF16)
                out[idx] = m_new
                out[idx + 1] = alpha * acc + _dot(vh, p)
        return tuple(out)

    def body(i, carry):
        for u in range(0, UNROLL, 2):
            logits(UNROLL * i + u + 1, sb_ref)
            carry = update(UNROLL * i + u, sa_ref, carry, None)
            logits(UNROLL * i + u + 2, sa_ref)
            carry = update(UNROLL * i + u + 1, sb_ref, carry, None)
        return carry

    first = qi * n_diag
    m0 = jnp.full((1, ck), NEG, F32)
    a0 = jnp.zeros((V_ROWS, ck), F32)
    logits(0, sa_ref)
    carry = lax.fori_loop(0, first // UNROLL, body, (m0, a0) * (2 * n_diag))
    bufs = (sa_ref, sb_ref)
    for dg in range(n_diag):
        if dg + 1 < n_diag:
            logits(first + dg + 1, bufs[(dg + 1) % 2], (dg + 1) * ck)
        carry = update(first + dg, bufs[dg % 2], carry, dg)
    accs = [jnp.concatenate([carry[2 * (hh * n_diag + blk) + 1] for blk in range(n_diag)], axis=1)
            for hh in range(2)]
    outs = [acc[:HEAD_DIM] * (1.0 / acc[HEAD_DIM:HEAD_DIM + 1]) for acc in accs]
    o_ref[0] = jnp.transpose(jnp.concatenate(outs, axis=0)).astype(BF16)


def _fox_flash(qt, kn, cb, vt_all, layer, tq, ck):
    b, s, _ = kn.shape
    hw = 2 * HEAD_DIM
    return pl.pallas_call(
        functools.partial(_fox_flash_kernel, tq=tq, ck=ck),
        grid=(b, D_TOK // hw, s // tq),
        in_specs=[pl.BlockSpec((1, hw, tq), lambda i, h, j: (i, h, j)),
                  pl.BlockSpec((1, s, hw), lambda i, h, j: (i, 0, h)),
                  pl.BlockSpec((1, s, LANES), lambda i, h, j: (i, 0, 0)),
                  pl.BlockSpec((1, 1, hw, s), lambda i, h, j: (layer, i, h, 0))],
        out_specs=pl.BlockSpec((1, tq, hw), lambda i, h, j: (i, j, h)),
        out_shape=jax.ShapeDtypeStruct((b, s, D_TOK), BF16),
        scratch_shapes=[pltpu.VMEM((2, V_ROWS, s), BF16),
                        pltpu.VMEM((ck, 2 * tq), F32), pltpu.VMEM((ck, 2 * tq), F32)],
        compiler_params=_params(3),
        name="fox_flash",
    )(qt, kn, cb, vt_all)


def _fox_decode_kernel(pt_ref, q_ref, kn_ref, vn_ref, lfn_ref, *refs, n_pages, t_new):
    del pt_ref
    kp = refs[:n_pages]
    vp = refs[n_pages:2 * n_pages]
    lp = refs[2 * n_pages:3 * n_pages]
    o_ref = refs[3 * n_pages]
    rows = t_new * HEADS_PAD

    q8 = q_ref[0]
    hrow = lax.broadcasted_iota(jnp.int32, (HEADS_PAD, D_TOK), 0)
    hcol = lax.broadcasted_iota(jnp.int32, (HEADS_PAD, D_TOK), 1) // HEAD_DIM
    head_mask = jnp.where(hrow == hcol, 1.0, 0.0)
    qs = jnp.concatenate([jnp.broadcast_to(q8[t:t + 1, :], (HEADS_PAD, D_TOK)) * head_mask
                          for t in range(t_new)], axis=0)
    qsb = qs.astype(BF16)

    x = jnp.concatenate([lp[p][0, 0] for p in range(n_pages)], axis=0)
    r = _dot_f32_lhs(x, _tri_ones())
    offs = jnp.zeros((HEADS_PAD, LANES), F32)
    s_pages = []
    for p in range(n_pages):
        c_page = r[p * HEADS_PAD:(p + 1) * HEADS_PAD, :LANES] + offs
        offs = offs + r[p * HEADS_PAD:(p + 1) * HEADS_PAD, LANES:]
        kpg = kp[p][0, 0].reshape(D_TOK, LANES).astype(BF16)
        s = _dot(qsb, kpg) * SCALE
        s_pages.append(s - jnp.concatenate([c_page] * t_new, axis=0))

    kn = kn_ref[0]
    vn = vn_ref[0]
    lfn = lfn_ref[0]
    trow = lax.broadcasted_iota(jnp.int32, (rows, 1), 0) // HEADS_PAD
    c_new = offs[:, :1]
    s_new = []
    for t in range(t_new):
        c_new = c_new + lfn[:, t:t + 1]
        st = jnp.sum(qs * kn[t:t + 1, :], axis=-1, keepdims=True) * SCALE
        st = st - jnp.concatenate([c_new] * t_new, axis=0)
        s_new.append(jnp.where(trow >= t, st, NEG))

    mx = s_pages[0]
    for p in range(1, n_pages):
        mx = jnp.maximum(mx, s_pages[p])
    m = jnp.max(mx, axis=-1, keepdims=True)
    for t in range(t_new):
        m = jnp.maximum(m, s_new[t])

    lsum = jnp.zeros((rows, LANES), F32)
    o = jnp.zeros((rows, D_TOK), F32)
    for p in range(n_pages):
        pp = jnp.exp(s_pages[p] - m)
        lsum = lsum + pp
        vpg = vp[p][0, 0].reshape(D_TOK, LANES).astype(BF16)
        o = o + _dot_nt(pp.astype(BF16), vpg)
    l = jnp.sum(lsum, axis=-1, keepdims=True)
    for t in range(t_new):
        pn = jnp.exp(s_new[t] - m)
        l = l + pn
        o = o + pn.astype(BF16).astype(F32) * vn[t:t + 1, :]
    o = o * (1.0 / l)
    outs = [jnp.sum(o[t * HEADS_PAD:(t + 1) * HEADS_PAD] * head_mask, axis=0, keepdims=True)
            for t in range(t_new)]
    outs.append(jnp.zeros((SUBLANES - t_new, D_TOK), F32))
    o_ref[0] = jnp.concatenate(outs, axis=0)


def _fox_decode(page_table_flat, q8, kn8, vn8, lfn, ck, cv, clf, layer, n_pages, t_new):
    db = q8.shape[0]
    page = ck.shape[-1]
    tok = pl.BlockSpec((1, SUBLANES, D_TOK), lambda i, pt: (i, 0, 0))
    kv_spec = lambda p: pl.BlockSpec((1, 1, FOX_HEADS, HEAD_DIM, page),
                                     lambda i, pt: (layer, pt[i * n_pages + p], 0, 0, 0))
    lf_spec = lambda p: pl.BlockSpec((1, 1, HEADS_PAD, page),
                                     lambda i, pt: (layer, pt[i * n_pages + p], 0, 0))
    grid_spec = pltpu.PrefetchScalarGridSpec(
        num_scalar_prefetch=1,
        grid=(db,),
        in_specs=[tok, tok, tok, pl.BlockSpec((1, HEADS_PAD, SUBLANES), lambda i, pt: (i, 0, 0))]
        + [kv_spec(p) for p in range(n_pages)] + [kv_spec(p) for p in range(n_pages)]
        + [lf_spec(p) for p in range(n_pages)],
        out_specs=tok,
    )
    return pl.pallas_call(
        functools.partial(_fox_decode_kernel, n_pages=n_pages, t_new=t_new),
        grid_spec=grid_spec,
        out_shape=jax.ShapeDtypeStruct((db, SUBLANES, D_TOK), F32),
        compiler_params=_params(1),
        name="fox_decode",
    )(page_table_flat, q8, kn8, vn8, lfn, *([ck] * n_pages), *([cv] * n_pages), *([clf] * n_pages))


def _lru_gates(xc, gate_refs, lam_ref):
    wr_ref, br_ref, wi_ref, bi_ref = gate_refs
    xcb = xc.astype(BF16)
    nblk = D_TOK // GATE_BLOCK
    sl = lambda i: xcb[:, i * GATE_BLOCK:(i + 1) * GATE_BLOCK]
    r_pre = jnp.concatenate([_dot(sl(i), wr_ref[i]) for i in range(nblk)], axis=1) + br_ref[...]
    i_pre = jnp.concatenate([_dot(sl(i), wi_ref[i]) for i in range(nblk)], axis=1) + bi_ref[...]
    r = jax.nn.sigmoid(r_pre)
    ig = jax.nn.sigmoid(i_pre)
    log_a = -LRU_C * r * _softplus(-lam_ref[...])
    a = jnp.exp(log_a)
    u = jnp.sqrt(-jnp.tanh(log_a) * (a * a + 1.0)) * ig * xc
    return a, u


def _lru_prompt_kernel(x_ref, g_ref, w_ref, cw_ref, cb_ref, wr_ref, br_ref, wi_ref, bi_ref, lam_ref,
                       y_ref, cq_ref, hl_ref, cs_ref, tail_ref, hc_ref, *, tm):
    @pl.when(pl.program_id(1) == 0)
    def _():
        tail_ref[...] = jnp.zeros_like(tail_ref)
        hc_ref[...] = jnp.zeros_like(hc_ref)

    h = _rmsnorm(x_ref[0], g_ref[...]).astype(BF16)
    z = _dot(h, w_ref[...])
    gate = z[:, :D_TOK]
    xb = z[:, D_TOK:2 * D_TOK]
    cq_ref[0] = z[:, 2 * D_TOK:].astype(BF16)
    ext = jnp.concatenate([tail_ref[...], xb], axis=0)
    xc = cb_ref[...] + xb * cw_ref[CONV_WIDTH - 1:CONV_WIDTH, :]
    for i in range(CONV_WIDTH - 1):
        back = CONV_WIDTH - 1 - i
        xc = xc + pltpu.roll(ext, back, axis=0)[SUBLANES:, :] * cw_ref[i:i + 1, :]
    tail_ref[...] = xb[tm - SUBLANES:, :]
    cs_ref[0] = xb[tm - (CONV_WIDTH - 1):, :]

    a, u = _lru_gates(xc, (wr_ref, br_ref, wi_ref, bi_ref), lam_ref)
    groups = tm // SUBLANES
    a3 = a.reshape(groups, SUBLANES, D_TOK)
    u3 = u.reshape(groups, SUBLANES, D_TOK)
    sub = lax.broadcasted_iota(jnp.int32, (groups, SUBLANES, D_TOK), 1)
    sh = 1
    while sh < SUBLANES:
        keep = sub >= sh
        a_sh = jnp.where(keep, pltpu.roll(a3, sh, axis=1), 1.0)
        u_sh = jnp.where(keep, pltpu.roll(u3, sh, axis=1), 0.0)
        u3 = u3 + a3 * u_sh
        a3 = a3 * a_sh
        sh *= 2
    hc = hc_ref[...]
    hs = []
    for gi in range(groups):
        hg = u3[gi] + a3[gi] * hc
        hs.append(hg)
        hc = hg[SUBLANES - 1:SUBLANES, :]
    hs = jnp.concatenate(hs, axis=0)
    hc_ref[...] = hc
    hl_ref[0] = hc
    y_ref[0] = (_gelu_tanh(gate) * hs).astype(BF16)


def _lru_prompt(x, g, w_all, cw, cb, wr, br, wi, bi, lam, layer, tm):
    b, s, d = x.shape
    tok = lambda wd: pl.BlockSpec((1, tm, wd), lambda i, j: (i, j, 0))
    consts = (g, w_all, cw, cb, wr, br, wi, bi, lam)
    specs = [_const_spec(c.shape) for c in consts]
    specs[1] = _layer_spec(w_all.shape, layer)
    return pl.pallas_call(
        functools.partial(_lru_prompt_kernel, tm=tm),
        grid=(b, s // tm),
        in_specs=[tok(d)] + specs,
        out_specs=[tok(D_TOK), tok(MEM_WIDTH),
                   pl.BlockSpec((1, 1, D_TOK), lambda i, j: (i, 0, 0)),
                   pl.BlockSpec((1, CONV_WIDTH - 1, D_TOK), lambda i, j: (i, 0, 0))],
        out_shape=[jax.ShapeDtypeStruct((b, s, D_TOK), BF16),
                   jax.ShapeDtypeStruct((b, s, MEM_WIDTH), BF16),
                   jax.ShapeDtypeStruct((b, 1, D_TOK), F32),
                   jax.ShapeDtypeStruct((b, CONV_WIDTH - 1, D_TOK), F32)],
        scratch_shapes=[pltpu.VMEM((SUBLANES, D_TOK), F32), pltpu.VMEM((1, D_TOK), F32)],
        compiler_params=_params(2),
        name="lru_prompt",
    )(x, *consts)


def _lru_sample_kernel(x_ref, g_ref, w_ref, cw_ref, cb_ref, wr_ref, br_ref, wi_ref, bi_ref, lam_ref,
                       h0_ref, buf_ref, y_ref, cq_ref, hl_ref, nb_ref, *, t_new, db):
    h = _rmsnorm(x_ref[...], g_ref[...]).astype(BF16)
    z = _dot(h, w_ref[...])
    gate = z[:, :D_TOK]
    xb = z[:, D_TOK:2 * D_TOK]
    cq_ref[...] = z[:, 2 * D_TOK:]
    xp = [buf_ref[i] for i in range(CONV_WIDTH - 1)] + [xb[t * db:(t + 1) * db] for t in range(t_new)]
    xc_t = []
    for t in range(t_new):
        acc = cb_ref[...] + xp[t] * cw_ref[0:1, :]
        for i in range(1, CONV_WIDTH):
            acc = acc + xp[t + i] * cw_ref[i:i + 1, :]
        xc_t.append(acc)
    for i in range(CONV_WIDTH - 1):
        nb_ref[i] = xp[t_new + i]
    xc = jnp.concatenate(xc_t, axis=0)
    a, u = _lru_gates(xc, (wr_ref, br_ref, wi_ref, bi_ref), lam_ref)
    hc = h0_ref[...]
    hs = []
    for t in range(t_new):
        hc = a[t * db:(t + 1) * db] * hc + u[t * db:(t + 1) * db]
        hs.append(hc)
    hl_ref[...] = hc
    y_ref[...] = _gelu_tanh(gate) * jnp.concatenate(hs, axis=0)


def _lru_sample(x, g, w_all, cw, cb, wr, br, wi, bi, lam, h0, buf, layer, t_new, db):
    n, d = x.shape
    ins = (x, g, w_all, cw, cb, wr, br, wi, bi, lam, h0, buf)
    outs = [jax.ShapeDtypeStruct((n, D_TOK), F32),
            jax.ShapeDtypeStruct((n, MEM_WIDTH), F32),
            jax.ShapeDtypeStruct((db, D_TOK), F32),
            jax.ShapeDtypeStruct((CONV_WIDTH - 1, db, D_TOK), F32)]
    full = lambda a: pl.BlockSpec(a.shape, lambda i: (0,) * len(a.shape))
    return pl.pallas_call(
        functools.partial(_lru_sample_kernel, t_new=t_new, db=db),
        grid=(1,),
        in_specs=[_layer_spec(a.shape, layer) if a is w_all else full(a) for a in ins],
        out_specs=[full(o) for o in outs],
        out_shape=outs,
        compiler_params=_params(1),
        name="lru_sample",
    )(*ins)


def _cross_sample_kernel(q_ref, mk_ref, mv_ref, o_ref, *, t_new, group):
    hpad = SUBLANES
    hrow = lax.broadcasted_iota(jnp.int32, (hpad, MEM_WIDTH), 0)
    hcol = lax.broadcasted_iota(jnp.int32, (hpad, MEM_WIDTH), 1) // HEAD_DIM
    head_mask = jnp.where(hrow == hcol, 1.0, 0.0)
    qs = [jnp.concatenate([jnp.broadcast_to(q_ref[i][t:t + 1, :], (hpad, MEM_WIDTH)) * head_mask
                           for t in range(t_new)], axis=0).astype(BF16) for i in range(group)]
    s = [_dot(qs[i], mk_ref[0, i].astype(BF16)) * SCALE for i in range(group)]
    e = [jnp.exp(si - jnp.max(si, axis=-1, keepdims=True)) for si in s]
    p = [(ei * (1.0 / jnp.sum(ei, axis=-1, keepdims=True))).astype(BF16) for ei in e]
    o = [_dot_nt(p[i], mv_ref[0, i].astype(BF16)) for i in range(group)]
    for i in range(group):
        outs = [jnp.sum(o[i][t * hpad:(t + 1) * hpad] * head_mask, axis=0, keepdims=True)
                for t in range(t_new)]
        outs.append(jnp.zeros((SUBLANES - t_new, MEM_WIDTH), F32))
        o_ref[i] = jnp.concatenate(outs, axis=0)


def _cross_sample(q8, cmk, cmv, layer, t_new):
    db = q8.shape[0]
    n_mem = cmk.shape[-1]
    group = SUBLANES
    tok = pl.BlockSpec((group, SUBLANES, MEM_WIDTH), lambda i: (i, 0, 0))
    kv = pl.BlockSpec((1, group, MEM_WIDTH, n_mem), lambda i: (layer, i, 0, 0))
    return pl.pallas_call(
        functools.partial(_cross_sample_kernel, t_new=t_new, group=group),
        grid=(db // group,),
        in_specs=[tok, kv, kv],
        out_specs=tok,
        out_shape=jax.ShapeDtypeStruct((db, SUBLANES, MEM_WIDTH), F32),
        compiler_params=_params(1),
        name="cross_sample",
    )(q8, cmk, cmv)


def _cross_attend(cq, mk, mv):
    tm = cq.shape[0]
    mk = mk.astype(BF16)
    mv = mv.astype(BF16)
    lane_head = lax.broadcasted_iota(jnp.int32, (tm, MEM_WIDTH), 1) // HEAD_DIM
    heads = range(MEM_HEADS)
    s = [_dot(jnp.where(lane_head == hd, cq, jnp.zeros_like(cq)), mk) * SCALE for hd in heads]
    e = [jnp.exp(sh - jnp.max(sh, axis=-1, keepdims=True)) for sh in s]
    p = [(eh * (1.0 / jnp.sum(eh, axis=-1, keepdims=True))).astype(BF16) for eh in e]
    cross = jnp.zeros((tm, MEM_WIDTH), F32)
    for hd in heads:
        cross = jnp.where(lane_head == hd, _dot_nt(p[hd], mv), cross)
    return cross.astype(BF16)


def _finish_kernel(*refs, attend, final):
    if attend:
        x_ref, yt_ref, c_ref, mk_ref, mv_ref, wo_ref, g_ref, wu_ref, wd_ref, gf_ref, o_ref = refs
        cross = _cross_attend(c_ref[0], mk_ref[...], mv_ref[...])
    else:
        x_ref, yt_ref, c_ref, wo_ref, g_ref, wu_ref, wd_ref, gf_ref, o_ref = refs
        cross = c_ref[0]
    y = _dot(yt_ref[0], wo_ref[:D_TOK, :]) + _dot(cross, wo_ref[D_TOK:, :])
    x1 = x_ref[0] + y
    hm = _rmsnorm(x1, g_ref[...]).astype(BF16)
    acc = x1
    d_ff = wu_ref.shape[1]
    for c in range(d_ff // FF_CHUNK):
        up = _dot(hm, wu_ref[:, c * FF_CHUNK:(c + 1) * FF_CHUNK])
        act = jnp.square(jnp.maximum(up, 0.0)).astype(BF16)
        acc = acc + _dot(act, wd_ref[c * FF_CHUNK:(c + 1) * FF_CHUNK, :])
    if final:
        acc = _rmsnorm(acc, gf_ref[...])
    o_ref[0] = acc


def _finish(x, yt, c, mk_all, mv_all, wo_all, g_all, wu_all, wd_all, gf, layer, tm, final):
    b, s, d = x.shape
    attend = mk_all is not None
    tok = lambda wdt: pl.BlockSpec((1, tm, wdt), lambda i, j: (i, j, 0))
    ins = [x, yt, c]
    in_specs = [tok(d), tok(D_TOK), tok(MEM_WIDTH)]
    if attend:
        ins += [mk_all, mv_all]
        in_specs += [pl.BlockSpec((None, None) + mk_all.shape[2:], lambda i, j: (layer, i, 0, 0))] * 2
    ins += [wo_all, g_all, wu_all, wd_all, gf]
    in_specs += [_layer_spec(wo_all.shape, layer), _layer_spec(g_all.shape, layer),
                 _layer_spec(wu_all.shape, layer), _layer_spec(wd_all.shape, layer), _const_spec(gf.shape)]
    return pl.pallas_call(
        functools.partial(_finish_kernel, attend=attend, final=final),
        grid=(b, s // tm),
        in_specs=in_specs,
        out_specs=tok(d),
        out_shape=jax.ShapeDtypeStruct((b, s, d), F32),
        compiler_params=_params(2),
        name="finish_prompt" if attend else "finish_sample",
    )(*ins)


def _pad_rows(a, rows):
    return jnp.pad(a, ((0, 0), (0, rows - a.shape[1]), (0, 0)))


def _block_diag(w):
    per = GATE_BLOCK // HEAD_DIM
    w = w.reshape(D_TOK // GATE_BLOCK, per, HEAD_DIM, HEAD_DIM)
    eye = jnp.eye(per, dtype=w.dtype)
    out = jnp.einsum('gpij,pq->gpiqj', w, eye)
    return out.reshape(D_TOK // GATE_BLOCK, GATE_BLOCK, GATE_BLOCK).astype(BF16)


def kernel(x_prompt, x_sample, mem_prompt, cache_fox_k, cache_fox_v, cache_fox_logf, state_lru_h, state_lru_conv, cache_mem_k, cache_mem_v, page_table, norm_mix, norm_mem, norm_mlp, norm_final, w_in_fox, b_forget, w_in_lru, conv_w, conv_b, w_rgate, b_rgate, w_igate, b_igate, lru_lambda, w_mem_kv, w_out, w_up, w_down):
    b, s, d = x_prompt.shape
    db, t_new, _ = x_sample.shape
    depth = norm_mix.shape[0]
    n_pages = page_table.shape[1]
    n_mem = mem_prompt.shape[1]
    tm = min(ROW_TILE, s)
    n_s = t_new * db
    assert s % tm == 0 and tm % LANES == 0 and t_new <= SUBLANES and db % SUBLANES == 0
    assert s % FLASH_TQ == 0 and FLASH_TQ % (UNROLL * FLASH_CK) == 0 and UNROLL % 2 == 0

    row = lambda v: v.reshape(1, -1)
    xp = x_prompt
    xs = jnp.transpose(x_sample, (1, 0, 2)).reshape(n_s, d)
    ck = jnp.transpose(cache_fox_k, (0, 1, 3, 4, 2))
    cv = jnp.transpose(cache_fox_v, (0, 1, 3, 4, 2))
    clf = jnp.pad(jnp.transpose(cache_fox_logf, (0, 1, 3, 2)),
                  ((0, 0), (0, 0), (0, HEADS_PAD - FOX_HEADS), (0, 0)))
    cmk = jnp.transpose(cache_mem_k, (0, 1, 3, 4, 2)).reshape(depth, db, MEM_WIDTH, n_mem)
    cmv = jnp.transpose(cache_mem_v, (0, 1, 3, 4, 2)).reshape(depth, db, MEM_WIDTH, n_mem)
    pt_flat = page_table.reshape(-1)

    mk_all, mv_all = _mem_kv(mem_prompt, norm_mem, jnp.transpose(w_mem_kv, (0, 2, 1)).astype(BF16))

    wo_all = w_out.astype(BF16)
    wu_all = w_up.astype(BF16)
    wd_all = w_down.astype(BF16)
    w_lru_all = w_in_lru.astype(BF16)
    g_mlp_all = norm_mlp.reshape(depth, 1, d)
    g_fin = row(norm_final)
    wall_t = jnp.transpose(w_in_fox, (2, 0, 1))
    n_qkv = 3 * D_TOK
    w_fox_all = jnp.transpose(jnp.concatenate(
        [wall_t[:n_qkv], wall_t[n_qkv + FOX_HEADS:],
         jnp.pad(wall_t[n_qkv:n_qkv + FOX_HEADS], ((0, HEADS_PAD - FOX_HEADS), (0, 0), (0, 0)))],
        axis=0), (1, 0, 2)).astype(BF16)

    def to_seq(a):
        return _pad_rows(jnp.transpose(a.reshape(t_new, db, -1), (1, 0, 2)), SUBLANES)

    def from_seq(a):
        return jnp.transpose(a[:, :t_new], (1, 0, 2)).reshape(n_s, -1).astype(BF16)

    rnd = lambda a: a.astype(BF16).astype(F32)
    pf, sk, sv, sf = [], [], [], []
    n_fox = (depth + 1) // 2
    kt_all = vt_all = None
    ph, pc, sh, sc = [], [], [], []
    for l in range(depth):
        j = l // 2
        final = l == depth - 1
        g_mix = row(norm_mix[l])
        fin_w = (wo_all, g_mlp_all, wu_all, wd_all, g_fin, l)
        mem_l = (mk_all, mv_all)
        if l % 2 == 0:
            bft = jnp.pad(b_forget[j], (0, HEADS_PAD - FOX_HEADS)).reshape(HEADS_PAD, 1)
            qt_p, kt_all, vt_all, kn_p, lft_p, cb_p, cq_p = _fox_in_prompt(
                xp, g_mix, w_fox_all, bft, tm, j, n_fox, kt_all, vt_all)
            yt_p = _fox_flash(qt_p, kn_p, cb_p, vt_all, j, FLASH_TQ, FLASH_CK)
            pf.append(jnp.transpose(lft_p[:, :FOX_HEADS], (0, 2, 1)))

            q_s, kn_s, vn_s, kt_s, vt_s, lft_s, cq_s = _fox_in_sample(xs, g_mix, w_fox_all, bft, j, t_new, db)
            lfn = jnp.pad(jnp.transpose(lft_s.reshape(HEADS_PAD, t_new, db), (2, 0, 1)),
                          ((0, 0), (0, 0), (0, SUBLANES - t_new)))
            xp = _finish(xp, yt_p, cq_p, *mem_l, *fin_w, tm, final)
            yt_s8 = _fox_decode(pt_flat, to_seq(rnd(q_s)), to_seq(rnd(kn_s)), to_seq(rnd(vn_s)), lfn,
                                ck, cv, clf, j, n_pages, t_new)
            yt_s = from_seq(yt_s8)
            sk.append(jnp.transpose(kt_s.reshape(t_new, FOX_HEADS, HEAD_DIM, db), (3, 0, 1, 2)))
            sv.append(jnp.transpose(vt_s.reshape(t_new, FOX_HEADS, HEAD_DIM, db), (3, 0, 1, 2)))
            sf.append(jnp.transpose(lft_s[:FOX_HEADS].reshape(FOX_HEADS, t_new, db), (2, 1, 0)))
        else:
            consts = (g_mix, w_lru_all, conv_w[j], row(conv_b[j]),
                      _block_diag(w_rgate[j]), row(b_rgate[j]),
                      _block_diag(w_igate[j]), row(b_igate[j]), row(lru_lambda[j]))
            yt_p, cq_p, hl_p, cs_p = _lru_prompt(xp, *consts, j, tm)
            ph.append(hl_p.reshape(b, D_TOK))
            pc.append(cs_p)
            yt_s, cq_s, hl_s, nb_s = _lru_sample(xs, *consts, state_lru_h[j],
                                                 jnp.transpose(state_lru_conv[j], (1, 0, 2)), j, t_new, db)
            yt_s = yt_s.astype(BF16)
            sh.append(hl_s)
            sc.append(jnp.transpose(nb_s, (1, 0, 2)))
            xp = _finish(xp, yt_p, cq_p, *mem_l, *fin_w, tm, final)

        cross_s = from_seq(_cross_sample(to_seq(rnd(cq_s)), cmk, cmv, l, t_new))
        xs = _finish(xs.reshape(1, n_s, d), yt_s.reshape(1, n_s, D_TOK), cross_s.reshape(1, n_s, MEM_WIDTH),
                     None, None, *fin_w, n_s, final).reshape(n_s, d)

    y_sample = jnp.transpose(xs.reshape(t_new, db, d), (1, 0, 2))
    mem_out = lambda m: jnp.transpose(m.reshape(depth, b, MEM_HEADS, HEAD_DIM, n_mem), (0, 1, 4, 2, 3))
    fox_out = lambda a: jnp.transpose(a.reshape(n_fox, b, FOX_HEADS, HEAD_DIM, s), (0, 1, 4, 2, 3))
    return (xp, y_sample,
            fox_out(kt_all), fox_out(vt_all), jnp.stack(pf),
            jnp.stack(sk), jnp.stack(sv), jnp.stack(sf),
            jnp.stack(ph), jnp.stack(pc), jnp.stack(sh), jnp.stack(sc),
            mem_out(mk_all), mem_out(mv_all))
```
